```python
import jax, jax.numpy as jnp
from jax import lax
import numpy as np

D_MODEL = 2048
BATCH = 8
SEQ = 2048
DEPTH = 1

DN_HEADS = 8
DN_HEAD_DIM = 128
DN_WIDTH = DN_HEADS * DN_HEAD_DIM
DN_CONV = 4
DN_CHUNK = 64
CF_WIDTH = 1024
CF_KERNEL = 31
FFN_DIM = 5632
FFN_CONV = 3
EPS = 1e-6

IN_SIZES = [3 * DN_WIDTH,
            DN_WIDTH,
            DN_HEADS,
            DN_HEADS,
            2 * CF_WIDTH,
            D_MODEL,
            D_MODEL]
N_IN = sum(IN_SIZES)
IN_SPLITS = [int(v) for v in np.cumsum(IN_SIZES)[:-1]]

kernel_name = "hybrid_deltanet_conformer_convffn_adaln"


def rmsnorm(x, g):
    xf = x.astype(jnp.float32)
    y = xf * lax.rsqrt(jnp.mean(xf * xf, axis=-1, keepdims=True) + EPS)
    return (y * g.astype(jnp.float32)).astype(x.dtype)


def layernorm(x, g, b):
    xf = x.astype(jnp.float32)
    mu = jnp.mean(xf, axis=-1, keepdims=True)
    xc = xf - mu
    y = xc * lax.rsqrt(jnp.mean(xc * xc, axis=-1, keepdims=True) + EPS)
    return (y * g.astype(jnp.float32) + b.astype(jnp.float32)).astype(x.dtype)


def l2norm(x):
    return x * lax.rsqrt(jnp.sum(x * x, axis=-1, keepdims=True) + EPS)


def causal_dwconv(x, w):
    k = w.shape[0]
    return lax.conv_general_dilated(
        x, w[:, None, :].astype(x.dtype), window_strides=(1,), padding=[(k - 1, 0)],
        dimension_numbers=("NWC", "WIO", "NWC"), feature_group_count=x.shape[-1])


def chunk_gated_delta_rule(q, k, v, g, beta):
    b, s, h, dk = q.shape
    dv = v.shape[-1]
    c = DN_CHUNK
    n = s // c

    def to_chunks(t):
        return jnp.moveaxis(t.reshape((b, n, c, h) + t.shape[3:]), 3, 1)

    q, k, v, g, beta = (to_chunks(t) for t in (q, k, v, g, beta))
    g = jnp.cumsum(g, axis=-1)
    causal = jnp.tril(jnp.ones((c, c), dtype=bool))
    strict = jnp.tril(jnp.ones((c, c), dtype=bool), -1)
    diff = g[..., :, None] - g[..., None, :]
    decay = jnp.where(causal, jnp.exp(jnp.where(causal, diff, 0.0)), 0.0)

    kb = k * beta[..., None]
    m = jnp.einsum("bhnid,bhnjd->bhnij", kb, k) * decay
    m = jnp.where(strict, m, 0.0) + jnp.eye(c, dtype=jnp.float32)
    rhs = jnp.concatenate([v * beta[..., None], kb * jnp.exp(g)[..., None]], axis=-1)
    sol = lax.linalg.triangular_solve(m, rhs, left_side=True, lower=True, unit_diagonal=True)
    u0, w = sol[..., :dv], sol[..., dv:]

    attn = jnp.einsum("bhnid,bhnjd->bhnij", q, k) * decay
    q_dec = q * jnp.exp(g)[..., None]
    g_last = g[..., -1]
    k_dec = k * jnp.exp(g_last[..., None] - g)[..., None]

    def step(state, inp):
        u0_i, w_i, attn_i, qd_i, kd_i, gl_i = inp
        u = u0_i - jnp.einsum("bhck,bhkv->bhcv", w_i, state)
        o = (jnp.einsum("bhck,bhkv->bhcv", qd_i, state)
             + jnp.einsum("bhij,bhjv->bhiv", attn_i, u))
        state = (state * jnp.exp(gl_i)[..., None, None]
                 + jnp.einsum("bhck,bhcv->bhkv", kd_i, u))
        return state, o

    xs = tuple(jnp.moveaxis(t, 2, 0) for t in (u0, w, attn, q_dec, k_dec, g_last))
    s0 = jnp.zeros((b, h, dk, dv), jnp.float32)
    _, o = lax.scan(step, s0, xs)
    o = jnp.moveaxis(o, 0, 2)
    return jnp.moveaxis(o, 1, 3).reshape(b, s, h, dv)


def hybrid_mixer(hn, w_in, dn_conv_w, dn_a_log, dn_dt_bias, dn_norm_g, dn_w_o,
                 cf_conv_w, cf_ln_g, cf_ln_b, cf_w_o, w_out):
    b, s, _ = hn.shape
    proj = hn @ w_in
    qkv, z, beta_logit, a_logit, glu_in, gate_a, gate_b = jnp.split(proj, IN_SPLITS, axis=-1)

    qkv = jax.nn.silu(causal_dwconv(qkv, dn_conv_w)).astype(jnp.float32)
    q, k, v = (t.reshape(b, s, DN_HEADS, DN_HEAD_DIM) for t in jnp.split(qkv, 3, axis=-1))
    q = l2norm(q) * (DN_HEAD_DIM ** -0.5)
    k = l2norm(k)
    beta = jax.nn.sigmoid(beta_logit.astype(jnp.float32))
    g = -jnp.exp(dn_a_log.astype(jnp.float32)) * jax.nn.softplus(
        a_logit.astype(jnp.float32) + dn_dt_bias.astype(jnp.float32))
    o = chunk_gated_delta_rule(q, k, v, g, beta)
    o = o * lax.rsqrt(jnp.mean(o * o, axis=-1, keepdims=True) + EPS) * dn_norm_g.astype(jnp.float32)
    o = o * jax.nn.silu(z.astype(jnp.float32).reshape(b, s, DN_HEADS, DN_HEAD_DIM))
    branch_a = o.reshape(b, s, DN_WIDTH).astype(hn.dtype) @ dn_w_o

    val, gl = jnp.split(glu_in, 2, axis=-1)
    u = val * jax.nn.sigmoid(gl)
    u = causal_dwconv(u, cf_conv_w)
    u = jax.nn.silu(layernorm(u, cf_ln_g, cf_ln_b))
    branch_b = u @ cf_w_o

    merged = jax.nn.sigmoid(gate_a) * branch_a + jax.nn.sigmoid(gate_b) * branch_b
    return merged @ w_out


def conv_glu_ffn(hn, w_up, conv_w, w_down):
    gate, up = jnp.split(hn @ w_up, 2, axis=-1)
    gate = causal_dwconv(gate, conv_w)
    return (jax.nn.silu(gate) * up) @ w_down


def setup_inputs(seed: int = 0) -> dict:
    key = jax.random.key(seed)
    ks = jax.random.split(key, 24)
    L, D = DEPTH, D_MODEL
    nrm = lambda k, shape, s: jax.random.normal(k, shape, jnp.float32) * s
    dt = jnp.exp(jax.random.uniform(ks[5], (L, DN_HEADS), jnp.float32,
                                    float(np.log(1e-3)), float(np.log(1e-1))))
    return {
        "x": nrm(ks[0], (BATCH, SEQ, D), 1.0),
        "c": nrm(ks[1], (BATCH, D), 1.0),
        "w_ada": nrm(ks[2], (L, D, 6 * D), 0.5 * D ** -0.5),
        "b_ada": nrm(ks[3], (L, 6 * D), 0.01),
        "norm1_g": 1.0 + nrm(ks[4], (L, D), 0.02),
        "w_in": nrm(ks[6], (L, D, N_IN), D ** -0.5),
        "dn_conv_w": nrm(ks[7], (L, DN_CONV, 3 * DN_WIDTH), DN_CONV ** -0.5),
        "dn_a_log": jnp.log(jax.random.uniform(ks[8], (L, DN_HEADS), jnp.float32, 1.0, 16.0)),
        "dn_dt_bias": jnp.log(jnp.expm1(dt)),
        "dn_norm_g": 1.0 + nrm(ks[9], (L, DN_HEAD_DIM), 0.02),
        "dn_w_o": nrm(ks[10], (L, DN_WIDTH, D), DN_WIDTH ** -0.5),
        "cf_conv_w": nrm(ks[11], (L, CF_KERNEL, CF_WIDTH), CF_KERNEL ** -0.5),
        "cf_ln_g": 1.0 + nrm(ks[12], (L, CF_WIDTH), 0.02),
        "cf_ln_b": nrm(ks[13], (L, CF_WIDTH), 0.02),
        "cf_w_o": nrm(ks[14], (L, CF_WIDTH, D), CF_WIDTH ** -0.5),
        "w_out": nrm(ks[15], (L, D, D), D ** -0.5),
        "norm2_g": 1.0 + nrm(ks[16], (L, D), 0.02),
        "ffn_w_up": nrm(ks[17], (L, D, 2 * FFN_DIM), D ** -0.5),
        "ffn_conv_w": nrm(ks[18], (L, FFN_CONV, FFN_DIM), FFN_CONV ** -0.5),
        "ffn_w_down": nrm(ks[19], (L, FFN_DIM, D), FFN_DIM ** -0.5),
        "final_norm_g": 1.0 + nrm(ks[20], (D,), 0.02),
    }


def reference(x, c, w_ada, b_ada, norm1_g, w_in, dn_conv_w, dn_a_log, dn_dt_bias, dn_norm_g,
              dn_w_o, cf_conv_w, cf_ln_g, cf_ln_b, cf_w_o, w_out, norm2_g, ffn_w_up, ffn_conv_w,
              ffn_w_down, final_norm_g):
    c_act = jax.nn.silu(c)
    for l in range(DEPTH):
        mod = c_act @ w_ada[l] + b_ada[l]
        sh1, sc1, gt1, sh2, sc2, gt2 = jnp.split(mod[:, None, :], 6, axis=-1)
        hn = rmsnorm(x, norm1_g[l]) * (1.0 + sc1) + sh1
        x = x + gt1 * hybrid_mixer(hn, w_in[l], dn_conv_w[l], dn_a_log[l], dn_dt_bias[l],
                                   dn_norm_g[l], dn_w_o[l], cf_conv_w[l], cf_ln_g[l],
                                   cf_ln_b[l], cf_w_o[l], w_out[l])
        hn = rmsnorm(x, norm2_g[l]) * (1.0 + sc2) + sh2
        x = x + gt2 * conv_glu_ffn(hn, ffn_w_up[l], ffn_conv_w[l], ffn_w_down[l])
    return rmsnorm(x, final_norm_g)
```

```python
import functools

import jax
import jax.numpy as jnp
from jax import lax
from jax.experimental import pallas as pl
from jax.experimental.pallas import tpu as pltpu

F32 = jnp.float32
BF16 = jnp.bfloat16

D_MODEL = 2048
DN_HEADS = 8
DN_HEAD_DIM = 128
DN_WIDTH = DN_HEADS * DN_HEAD_DIM
DN_CONV = 4
DN_CHUNK = 64
CF_WIDTH = 1024
CF_KERNEL = 31
FFN_DIM = 5632
FFN_CONV = 3
EPS = 1e-6

LANES = 128
SUBLANES = 8
VMEM_LIMIT = 56 * 1024 * 1024

COL_Z = 3 * DN_WIDTH
COL_GLU = COL_Z + DN_WIDTH
COL_GATE_A = COL_GLU + 2 * CF_WIDTH
COL_GATE_B = COL_GATE_A + D_MODEL
N_MAIN = COL_GATE_B + D_MODEL

ADA_TN = 1024
INPROJ_TM = 1024
INPROJ_TN = 1024
DN_T = 256
CF_T = 256
CF_HALO = 32
MERGE_TM = 512
FFN_TM = 512
FFN_TF = 512


def _dot(a, b):
    return jnp.dot(a, b, preferred_element_type=F32)


def _sigmoid(x):
    return 1.0 / (1.0 + jnp.exp(-x))


def _silu(x):
    return x * _sigmoid(x)


def _params(sem):
    return pltpu.CompilerParams(dimension_semantics=sem, vmem_limit_bytes=VMEM_LIMIT)


def _ada_kernel(c_ref, w_ref, b_ref, o_ref):
    c = c_ref[...]
    cact = _silu(c).astype(BF16)
    o_ref[...] = _dot(cact, w_ref[...].astype(BF16)) + b_ref[...]


def _ada(c, w, b):
    bsz, d = c.shape
    n = w.shape[1]
    return pl.pallas_call(
        _ada_kernel,
        grid=(n // ADA_TN,),
        in_specs=[pl.BlockSpec((bsz, d), lambda j: (0, 0)),
                  pl.BlockSpec((d, ADA_TN), lambda j: (0, j)),
                  pl.BlockSpec((1, ADA_TN), lambda j: (0, j))],
        out_specs=pl.BlockSpec((bsz, ADA_TN), lambda j: (0, j)),
        out_shape=jax.ShapeDtypeStruct((bsz, n), F32),
        compiler_params=_params(("arbitrary",)),
        name="ada",
    )(c, w, b.reshape(1, n))


def _modulated_rmsnorm(x, g, shift, scale):
    ms = jnp.mean(x * x, axis=-1, keepdims=True)
    y = x * lax.rsqrt(ms + EPS) * g
    return y * (1.0 + scale) + shift


def _inproj_kernel(x_ref, mod_ref, g_ref, w_ref, wba_ref, out_ref, ba_ref, hn_ref):
    @pl.when(pl.program_id(1) == 0)
    def _():
        rows = 256
        for r in range(0, INPROJ_TM, rows):
            hn = _modulated_rmsnorm(x_ref[r:r + rows, :], g_ref[...], mod_ref[0:1, :], mod_ref[1:2, :])
            hn_ref[r:r + rows, :] = hn.astype(BF16)
        ba_ref[...] = _dot(hn_ref[...], wba_ref[...])

    out_ref[...] = _dot(hn_ref[...], w_ref[...]).astype(BF16)


def _inproj(x2d, mod6, g1, w_main, wba, seq):
    m, d = x2d.shape
    n = w_main.shape[1]
    tiles_per_seq = seq // INPROJ_TM
    return pl.pallas_call(
        _inproj_kernel,
        grid=(m // INPROJ_TM, n // INPROJ_TN),
        in_specs=[pl.BlockSpec((INPROJ_TM, d), lambda i, j: (i, 0)),
                  pl.BlockSpec((None, 6, d), lambda i, j: (i // tiles_per_seq, 0, 0)),
                  pl.BlockSpec((1, d), lambda i, j: (0, 0)),
                  pl.BlockSpec((d, INPROJ_TN), lambda i, j: (0, j)),
                  pl.BlockSpec((d, LANES), lambda i, j: (0, 0))],
        out_specs=[pl.BlockSpec((INPROJ_TM, INPROJ_TN), lambda i, j: (i, j)),
                   pl.BlockSpec((INPROJ_TM, LANES), lambda i, j: (i, 0))],
        out_shape=[jax.ShapeDtypeStruct((m, n), BF16),
                   jax.ShapeDtypeStruct((m, LANES), F32)],
        scratch_shapes=[pltpu.VMEM((INPROJ_TM, d), BF16)],
        compiler_params=_params(("arbitrary", "arbitrary")),
        name="inproj",
    )(x2d, mod6, g1, w_main, wba)


def _split3(x):
    h1 = x.astype(BF16)
    r1 = x - h1.astype(F32)
    h2 = r1.astype(BF16)
    h3 = (r1 - h2.astype(F32)).astype(BF16)
    return h1, h2, h3


def _deltanet_kernel(q_ref, k_ref, v_ref, z_ref, ba_ref, cw_ref, alog_ref, dtb_ref, ng_ref,
                     out_ref, xbuf, state):
    t = pl.program_id(1)
    T, C, H, DH = DN_T, DN_CHUNK, DN_HEADS, DN_HEAD_DIM
    NC = T // C
    W = DN_WIDTH

    @pl.when(t == 0)
    def _():
        xbuf[0:SUBLANES, :] = jnp.zeros((SUBLANES, 3 * W), F32)
        state[...] = jnp.zeros_like(state)

    @pl.when(t > 0)
    def _():
        xbuf[0:SUBLANES, :] = xbuf[T:T + SUBLANES, :]

    xbuf[SUBLANES:SUBLANES + T, 0:W] = q_ref[...].astype(F32)
    xbuf[SUBLANES:SUBLANES + T, W:2 * W] = k_ref[...].astype(F32)
    xbuf[SUBLANES:SUBLANES + T, 2 * W:3 * W] = v_ref[...].astype(F32)

    ba = ba_ref[...]
    beta_all = _sigmoid(ba)
    sp_in = ba + dtb_ref[...]
    softplus = jnp.maximum(sp_in, 0.0) + jnp.log1p(jnp.exp(-jnp.abs(sp_in)))
    g_all = -jnp.exp(alog_ref[...]) * softplus

    row = lax.broadcasted_iota(jnp.int32, (T, T), 0)
    col = lax.broadcasted_iota(jnp.int32, (T, T), 1)
    same = (row // C) == (col // C)
    causal = jnp.logical_and(same, col <= row)
    strict = jnp.logical_and(same, col < row)
    lower = jnp.where(causal, 1.0, 0.0).astype(BF16)
    upper = jnp.where(jnp.logical_and(same, row <= col), 1.0, 0.0).astype(BF16)

    g1, g2, g3 = _split3(g_all)
    gc_col_all = _dot(lower, g1) + _dot(lower, g2) + _dot(lower, g3)
    g_rows = g_all.T[0:2 * H, :]
    r1, r2, r3 = _split3(g_rows)
    gc_rows = _dot(r1, upper) + _dot(r2, upper) + _dot(r3, upper)
    eg_col_all = jnp.exp(gc_col_all)

    def conv(base, h):
        lo = base + h * DH
        acc = None
        for j in range(DN_CONV):
            term = xbuf[SUBLANES - (DN_CONV - 1) + j:SUBLANES - (DN_CONV - 1) + j + T, lo:lo + DH] \
                * cw_ref[j:j + 1, lo:lo + DH]
            acc = term if acc is None else acc + term
        return _silu(acc)

    def l2n(x):
        return x * lax.rsqrt(jnp.sum(x * x, axis=-1, keepdims=True) + EPS)

    u0s, wqs, aks, egs = [], [], [], []
    for h in range(H):
        q = l2n(conv(0, h)) * (DH ** -0.5)
        k = l2n(conv(W, h))
        v = conv(2 * W, h)
        beta = beta_all[:, h:h + 1]
        gcol = gc_col_all[:, H + h:H + h + 1]
        egcol = eg_col_all[:, H + h:H + h + 1]
        grow = gc_rows[H + h:H + h + 1, :]
        dec = jnp.exp(jnp.where(causal, gcol - grow, -1e30))

        kb = k * beta
        kT = k.T
        kTb = kT.astype(BF16)
        mm = _dot(jnp.concatenate([kb, q], axis=0).astype(BF16), kTb)
        a = jnp.where(strict, mm[:T] * dec, 0.0)
        attn = mm[T:] * dec

        ab = a.astype(BF16)
        r = -a
        p = _dot(ab, ab)
        n_sq = 5
        for s in range(n_sq):
            pb = p.astype(BF16)
            if s < n_sq - 1:
                rp = _dot(jnp.concatenate([r.astype(BF16), pb], axis=0), pb)
                r = r + p + rp[:T]
                p = rp[T:]
            else:
                r = r + p + _dot(r.astype(BF16), pb)

        rhs = jnp.concatenate([v * beta, kb * egcol], axis=1)
        sol = rhs + _dot(r.astype(BF16), rhs.astype(BF16))
        u0 = sol[:, :DH]
        w = sol[:, DH:]
        qd = q * egcol

        wq_h, ak_h, eg_h = [], [], []
        for c in range(NC):
            rs = slice(c * C, (c + 1) * C)
            pair = slice((c // 2) * LANES, (c // 2) * LANES + LANES)
            last = c * C + C - 1
            wq_h.append(jnp.concatenate([w[rs], qd[rs]], axis=0).astype(BF16))
            kd_t = kT[:, pair] * dec[last:last + 1, pair]
            ak_h.append(jnp.concatenate([attn[rs, pair], kd_t], axis=0).astype(BF16))
            eg_h.append(jnp.exp(gc_col_all[last:last + 1, H + h:H + h + 1]))
        u0s.append(u0)
        wqs.append(wq_h)
        aks.append(ak_h)
        egs.append(eg_h)

    outs = [[None] * NC for _ in range(H)]
    u_prev = [None] * H
    for c in range(NC):
        for h in range(H):
            s_h = state[h]
            res = _dot(wqs[h][c], s_h.astype(BF16))
            u = u0s[h][c * C:(c + 1) * C] - res[:C]
            if c % 2 == 0:
                u_pair = jnp.concatenate([u, jnp.zeros_like(u)], axis=0)
            else:
                u_pair = jnp.concatenate([u_prev[h], u], axis=0)
            u_prev[h] = u
            res2 = _dot(aks[h][c], u_pair.astype(BF16))
            outs[h][c] = res[C:] + res2[:C]
            state[h] = s_h * egs[h][c] + res2[C:]

    for h in range(H):
        o = jnp.concatenate(outs[h], axis=0)
        o = o * lax.rsqrt(jnp.mean(o * o, axis=-1, keepdims=True) + EPS) * ng_ref[...]
        zz = z_ref[:, h * DH:(h + 1) * DH].astype(F32)
        out_ref[:, h * DH:(h + 1) * DH] = (o * _silu(zz)).astype(BF16)


def _deltanet(proj, ba, conv_w, a_log, dt_bias, norm_g, bsz, seq):
    nt = seq // DN_T
    w = DN_WIDTH
    pad = lambda p: jnp.zeros((1, LANES), F32).at[0, DN_HEADS:2 * DN_HEADS].set(p)
    tok = lambda b, t: b * nt + t
    return pl.pallas_call(
        _deltanet_kernel,
        grid=(bsz, nt),
        in_specs=[pl.BlockSpec((DN_T, w), lambda b, t: (tok(b, t), 0)),
                  pl.BlockSpec((DN_T, w), lambda b, t: (tok(b, t), 1)),
                  pl.BlockSpec((DN_T, w), lambda b, t: (tok(b, t), 2)),
                  pl.BlockSpec((DN_T, w), lambda b, t: (tok(b, t), COL_Z // w)),
                  pl.BlockSpec((DN_T, LANES), lambda b, t: (tok(b, t), 0)),
                  pl.BlockSpec((DN_CONV, 3 * w), lambda b, t: (0, 0)),
                  pl.BlockSpec((1, LANES), lambda b, t: (0, 0)),
                  pl.BlockSpec((1, LANES), lambda b, t: (0, 0)),
                  pl.BlockSpec((1, DN_HEAD_DIM), lambda b, t: (0, 0))],
        out_specs=pl.BlockSpec((DN_T, w), lambda b, t: (tok(b, t), 0)),
        out_shape=jax.ShapeDtypeStruct((bsz * seq, w), BF16),
        scratch_shapes=[pltpu.VMEM((DN_T + SUBLANES, 3 * w), F32),
                        pltpu.VMEM((DN_HEADS, DN_HEAD_DIM, DN_HEAD_DIM), F32)],
        compiler_params=_params(("arbitrary", "arbitrary")),
        name="deltanet",
    )(proj, proj, proj, proj, ba, conv_w, pad(a_log), pad(dt_bias), norm_g.reshape(1, DN_HEAD_DIM))


def _conformer_kernel(val_ref, gl_ref, cw_ref, lg_ref, lb_ref, out_ref, ubuf, cbuf):
    t = pl.program_id(1)
    T = CF_T

    @pl.when(t == 0)
    def _():
        ubuf[0:CF_HALO, :] = jnp.zeros((CF_HALO, CF_WIDTH), F32)

    @pl.when(t > 0)
    def _():
        ubuf[0:CF_HALO, :] = ubuf[T:T + CF_HALO, :]

    ubuf[CF_HALO:CF_HALO + T, :] = val_ref[...].astype(F32) * _sigmoid(gl_ref[...].astype(F32))

    rows = 64
    off = CF_HALO - (CF_KERNEL - 1)
    for cb in range(CF_WIDTH // LANES):
        cs = slice(cb * LANES, (cb + 1) * LANES)
        for r0 in range(0, T, rows):
            acc = None
            for j in range(CF_KERNEL):
                term = ubuf[r0 + off + j:r0 + off + j + rows, cs] * cw_ref[j:j + 1, cs]
                acc = term if acc is None else acc + term
            cbuf[r0:r0 + rows, cs] = acc

    y = cbuf[...]
    mu = jnp.mean(y, axis=-1, keepdims=True)
    yc = y - mu
    var = jnp.mean(yc * yc, axis=-1, keepdims=True)
    yn = yc * lax.rsqrt(var + EPS) * lg_ref[...] + lb_ref[...]
    out_ref[...] = _silu(yn).astype(BF16)


def _conformer(proj, conv_w, ln_g, ln_b, bsz, seq):
    nt = seq // CF_T
    tok = lambda b, t: b * nt + t
    return pl.pallas_call(
        _conformer_kernel,
        grid=(bsz, nt),
        in_specs=[pl.BlockSpec((CF_T, CF_WIDTH), lambda b, t: (tok(b, t), COL_GLU // CF_WIDTH)),
                  pl.BlockSpec((CF_T, CF_WIDTH), lambda b, t: (tok(b, t), COL_GLU // CF_WIDTH + 1)),
                  pl.BlockSpec((CF_KERNEL, CF_WIDTH), lambda b, t: (0, 0)),
                  pl.BlockSpec((1, CF_WIDTH), lambda b, t: (0, 0)),
                  pl.BlockSpec((1, CF_WIDTH), lambda b, t: (0, 0))],
        out_specs=pl.BlockSpec((CF_T, CF_WIDTH), lambda b, t: (tok(b, t), 0)),
        out_shape=jax.ShapeDtypeStruct((bsz * seq, CF_WIDTH), BF16),
        scratch_shapes=[pltpu.VMEM((CF_T + CF_HALO, CF_WIDTH), F32),
                        pltpu.VMEM((CF_T, CF_WIDTH), F32)],
        compiler_params=_params(("arbitrary", "arbitrary")),
        name="conformer",
    )(proj, proj, conv_w, ln_g.reshape(1, CF_WIDTH), ln_b.reshape(1, CF_WIDTH))


def _merge_kernel(o_ref, u_ref, ga_ref, gb_ref, x_ref, mod_ref, g2_ref, wdo_ref, wco_ref, wout_ref,
                  x1_ref, hn2_ref):
    branch_a = _dot(o_ref[...], wdo_ref[...])
    branch_b = _dot(u_ref[...], wco_ref[...])
    merged = (_sigmoid(ga_ref[...].astype(F32)) * branch_a
              + _sigmoid(gb_ref[...].astype(F32)) * branch_b)
    mix = _dot(merged.astype(BF16), wout_ref[...])
    x1 = x_ref[...] + mod_ref[2:3, :] * mix
    x1_ref[...] = x1
    hn2 = _modulated_rmsnorm(x1, g2_ref[...], mod_ref[3:4, :], mod_ref[4:5, :])
    hn2_ref[...] = hn2.astype(BF16)


def _merge(o_gated, u_act, proj, x2d, mod6, g2, w_dn_o, w_cf_o, w_out, seq):
    m, d = x2d.shape
    tiles_per_seq = seq // MERGE_TM
    const = lambda shape: pl.BlockSpec(shape, lambda i: (0, 0), pipeline_mode=pl.Buffered(1))
    return pl.pallas_call(
        _merge_kernel,
        grid=(m // MERGE_TM,),
        in_specs=[pl.BlockSpec((MERGE_TM, DN_WIDTH), lambda i: (i, 0)),
                  pl.BlockSpec((MERGE_TM, CF_WIDTH), lambda i: (i, 0)),
                  pl.BlockSpec((MERGE_TM, d), lambda i: (i, COL_GATE_A // d)),
                  pl.BlockSpec((MERGE_TM, d), lambda i: (i, COL_GATE_B // d)),
                  pl.BlockSpec((MERGE_TM, d), lambda i: (i, 0)),
                  pl.BlockSpec((None, 6, d), lambda i: (i // tiles_per_seq, 0, 0)),
                  pl.BlockSpec((1, d), lambda i: (0, 0)),
                  const((DN_WIDTH, d)),
                  const((CF_WIDTH, d)),
                  const((d, d))],
        out_specs=[pl.BlockSpec((MERGE_TM, d), lambda i: (i, 0)),
                   pl.BlockSpec((MERGE_TM, d), lambda i: (i, 0))],
        out_shape=[jax.ShapeDtypeStruct((m, d), F32),
                   jax.ShapeDtypeStruct((m, d), BF16)],
        compiler_params=_params(("arbitrary",)),
        name="merge",
    )(o_gated, u_act, proj, proj, x2d, mod6, g2, w_dn_o, w_cf_o, w_out)


def _ffn_kernel(hn_ref, x1_ref, mod_ref, gf_ref, wg_ref, wu_ref, cw_ref, wd_ref, out_ref,
                gbuf, carry, *, tiles_per_seq):
    i = pl.program_id(0)
    j = pl.program_id(1)
    nf = pl.num_programs(1)
    TM = FFN_TM

    hn = hn_ref[...]
    gate = _dot(hn, wg_ref[...])
    up = _dot(hn, wu_ref[...])

    @pl.when(i % tiles_per_seq == 0)
    def _():
        gbuf[0:SUBLANES, :] = jnp.zeros((SUBLANES, FFN_TF), F32)

    @pl.when(i % tiles_per_seq != 0)
    def _():
        gbuf[0:SUBLANES, :] = carry[j]

    gbuf[SUBLANES:SUBLANES + TM, :] = gate
    carry[j] = gate[TM - SUBLANES:, :]

    acc = None
    for tap in range(FFN_CONV):
        o = SUBLANES - (FFN_CONV - 1) + tap
        term = gbuf[o:o + TM, :] * cw_ref[tap:tap + 1, :]
        acc = term if acc is None else acc + term
    h = (_silu(acc) * up).astype(BF16)
    part = _dot(h, wd_ref[...])

    @pl.when(j == 0)
    def _():
        out_ref[...] = part

    @pl.when(j > 0)
    def _():
        out_ref[...] += part

    @pl.when(j == nf - 1)
    def _():
        x2 = x1_ref[...] + mod_ref[5:6, :] * out_ref[...]
        ms = jnp.mean(x2 * x2, axis=-1, keepdims=True)
        out_ref[...] = x2 * lax.rsqrt(ms + EPS) * gf_ref[...]


def _ffn(hn2, x1, mod6, gf, w_up, conv_w, w_down, seq):
    m, d = x1.shape
    nf = FFN_DIM // FFN_TF
    tiles_per_seq = seq // FFN_TM
    return pl.pallas_call(
        functools.partial(_ffn_kernel, tiles_per_seq=tiles_per_seq),
        grid=(m // FFN_TM, nf),
        in_specs=[pl.BlockSpec((FFN_TM, d), lambda i, j: (i, 0)),
                  pl.BlockSpec((FFN_TM, d), lambda i, j: (i, 0)),
                  pl.BlockSpec((None, 6, d), lambda i, j: (i // tiles_per_seq, 0, 0)),
                  pl.BlockSpec((1, d), lambda i, j: (0, 0)),
                  pl.BlockSpec((d, FFN_TF), lambda i, j: (0, j)),
                  pl.BlockSpec((d, FFN_TF), lambda i, j: (0, nf + j)),
                  pl.BlockSpec((FFN_CONV, FFN_TF), lambda i, j: (0, j)),
                  pl.BlockSpec((FFN_TF, d), lambda i, j: (j, 0))],
        out_specs=pl.BlockSpec((FFN_TM, d), lambda i, j: (i, 0)),
        out_shape=jax.ShapeDtypeStruct((m, d), F32),
        scratch_shapes=[pltpu.VMEM((FFN_TM + SUBLANES, FFN_TF), F32),
                        pltpu.VMEM((nf, SUBLANES, FFN_TF), F32)],
        compiler_params=_params(("arbitrary", "arbitrary")),
        name="ffn",
    )(hn2, x1, mod6, gf, w_up, w_up, conv_w, w_down)


def kernel(x, c, w_ada, b_ada, norm1_g, w_in, dn_conv_w, dn_a_log, dn_dt_bias, dn_norm_g, dn_w_o,
           cf_conv_w, cf_ln_g, cf_ln_b, cf_w_o, w_out, norm2_g, ffn_w_up, ffn_conv_w, ffn_w_down,
           final_norm_g):
    bsz, seq, d = x.shape
    assert w_ada.shape[0] == 1, "single-layer block"
    x2d = x.reshape(bsz * seq, d)
    for l in range(1):
        mod6 = _ada(c, w_ada[l], b_ada[l]).reshape(bsz, 6, d)
        ba_lo = COL_GLU
        ba_hi = COL_GLU + 2 * DN_HEADS
        w_main = jnp.concatenate([w_in[l][:, :ba_lo], w_in[l][:, ba_hi:]], axis=1).astype(BF16)
        wba = jnp.pad(w_in[l][:, ba_lo:ba_hi], ((0, 0), (0, LANES - 2 * DN_HEADS))).astype(BF16)
        proj, ba = _inproj(x2d, mod6, norm1_g[l].reshape(1, d), w_main, wba, seq)
        o_gated = _deltanet(proj, ba, dn_conv_w[l], dn_a_log[l], dn_dt_bias[l], dn_norm_g[l], bsz, seq)
        u_act = _conformer(proj, cf_conv_w[l], cf_ln_g[l], cf_ln_b[l], bsz, seq)
        x1, hn2 = _merge(o_gated, u_act, proj, x2d, mod6, norm2_g[l].reshape(1, d),
                         dn_w_o[l].astype(BF16), cf_w_o[l].astype(BF16), w_out[l].astype(BF16), seq)
        x2d = _ffn(hn2, x1, mod6, final_norm_g.reshape(1, d), ffn_w_up[l].astype(BF16), ffn_conv_w[l],
                   ffn_w_down[l].astype(BF16), seq)
    return x2d.reshape(bsz, seq, d)
```

```python
import functools

import jax
import jax.numpy as jnp
from jax import lax
from jax.experimental import pallas as pl
from jax.experimental.pallas import tpu as pltpu

F32 = jnp.float32
BF16 = jnp.bfloat16

D_MODEL = 2048
DN_HEADS = 8
DN_HEAD_DIM = 128
DN_WIDTH = DN_HEADS * DN_HEAD_DIM
DN_CONV = 4
DN_CHUNK = 64
CF_WIDTH = 1024
CF_KERNEL = 31
FFN_DIM = 5632
FFN_CONV = 3
EPS = 1e-6

LANES = 128
SUBLANES = 8
VMEM_LIMIT = 56 * 1024 * 1024

COL_Z = 3 * DN_WIDTH
COL_GLU = COL_Z + DN_WIDTH
COL_GATE_A = COL_GLU + 2 * CF_WIDTH
COL_GATE_B = COL_GATE_A + D_MODEL
N_MAIN = COL_GATE_B + D_MODEL

ROW_SUB = 256
REPACK_ROWS = 256
ADA_TN = 1024
INPROJ_TM = 1024
INPROJ_TN = 1024
DN_T = 256
CF_T = 256
CF_HALO = 32
MERGE_TM = 512
FFN_TM = 512
FFN_TF = 512


def _dot(a, b):
    return jnp.dot(a, b, preferred_element_type=F32)


def _sigmoid(x):
    return 0.5 * jnp.tanh(0.5 * x) + 0.5


def _silu(x):
    h = 0.5 * x
    return h * jnp.tanh(h) + h


def _params(sem):
    return pltpu.CompilerParams(dimension_semantics=sem, vmem_limit_bytes=VMEM_LIMIT)


def _ada_kernel(c_ref, w_ref, b_ref, o_ref):
    c = c_ref[...]
    cact = _silu(c).astype(BF16)
    o_ref[...] = _dot(cact, w_ref[...].astype(BF16)) + b_ref[...]


def _ada(c, w, b):
    bsz, d = c.shape
    n = w.shape[1]
    return pl.pallas_call(
        _ada_kernel,
        grid=(n // ADA_TN,),
        in_specs=[pl.BlockSpec((bsz, d), lambda j: (0, 0)),
                  pl.BlockSpec((d, ADA_TN), lambda j: (0, j)),
                  pl.BlockSpec((1, ADA_TN), lambda j: (0, j))],
        out_specs=pl.BlockSpec((bsz, ADA_TN), lambda j: (0, j)),
        out_shape=jax.ShapeDtypeStruct((bsz, n), F32),
        compiler_params=_params(("arbitrary",)),
        name="ada",
    )(c, w, b.reshape(1, n))


def _repack_kernel(w_ref, main_ref, ba_ref):
    lo, hi = COL_GLU, COL_GLU + 2 * DN_HEADS
    main_ref[:, :lo] = w_ref[:, :lo].astype(BF16)
    main_ref[:, lo:] = w_ref[:, hi:].astype(BF16)
    lane = lax.broadcasted_iota(jnp.int32, (REPACK_ROWS, LANES), 1)
    ba_ref[...] = jnp.where(lane < 2 * DN_HEADS, w_ref[:, lo:lo + LANES], 0.0).astype(BF16)


def _repack_w_in(w):
    d, n = w.shape
    return pl.pallas_call(
        _repack_kernel,
        grid=(d // REPACK_ROWS,),
        in_specs=[pl.BlockSpec((REPACK_ROWS, n), lambda i: (i, 0))],
        out_specs=[pl.BlockSpec((REPACK_ROWS, N_MAIN), lambda i: (i, 0)),
                   pl.BlockSpec((REPACK_ROWS, LANES), lambda i: (i, 0))],
        out_shape=[jax.ShapeDtypeStruct((d, N_MAIN), BF16),
                   jax.ShapeDtypeStruct((d, LANES), BF16)],
        compiler_params=_params(("arbitrary",)),
        name="repack",
    )(w)


def _modulated_rmsnorm(x, g, shift, scale):
    ms = jnp.mean(x * x, axis=-1, keepdims=True)
    y = x * lax.rsqrt(ms + EPS) * g
    return y * (1.0 + scale) + shift


def _inproj_kernel(x_ref, mod_ref, g_ref, w_ref, wba_ref, out_ref, ba_ref, hn_ref):
    @pl.when(pl.program_id(1) == 0)
    def _():
        rows = 256
        for r in range(0, INPROJ_TM, rows):
            hn = _modulated_rmsnorm(x_ref[r:r + rows, :], g_ref[...], mod_ref[0:1, :], mod_ref[1:2, :])
            hn_ref[r:r + rows, :] = hn.astype(BF16)
        ba_ref[...] = _dot(hn_ref[...], wba_ref[...])

    out_ref[...] = _dot(hn_ref[...], w_ref[...]).astype(BF16)


def _inproj(x2d, mod6, g1, w_main, wba, seq):
    m, d = x2d.shape
    n = w_main.shape[1]
    tiles_per_seq = seq // INPROJ_TM
    return pl.pallas_call(
        _inproj_kernel,
        grid=(m // INPROJ_TM, n // INPROJ_TN),
        in_specs=[pl.BlockSpec((INPROJ_TM, d), lambda i, j: (i, 0)),
                  pl.BlockSpec((None, 6, d), lambda i, j: (i // tiles_per_seq, 0, 0)),
                  pl.BlockSpec((1, d), lambda i, j: (0, 0)),
                  pl.BlockSpec((d, INPROJ_TN), lambda i, j: (0, j)),
                  pl.BlockSpec((d, LANES), lambda i, j: (0, 0))],
        out_specs=[pl.BlockSpec((INPROJ_TM, INPROJ_TN), lambda i, j: (i, j)),
                   pl.BlockSpec((INPROJ_TM, LANES), lambda i, j: (i, 0))],
        out_shape=[jax.ShapeDtypeStruct((m, n), BF16),
                   jax.ShapeDtypeStruct((m, LANES), F32)],
        scratch_shapes=[pltpu.VMEM((INPROJ_TM, d), BF16)],
        compiler_params=_params(("arbitrary", "arbitrary")),
        name="inproj",
    )(x2d, mod6, g1, w_main, wba)


def _split3(x):
    h1 = x.astype(BF16)
    r1 = x - h1.astype(F32)
    h2 = r1.astype(BF16)
    h3 = (r1 - h2.astype(F32)).astype(BF16)
    return h1, h2, h3


def _deltanet_kernel(q_ref, k_ref, v_ref, z_ref, ba_ref, cw_ref, alog_ref, dtb_ref, ng_ref,
                     out_ref, xbuf, state):
    t = pl.program_id(1)
    T, C, H, DH = DN_T, DN_CHUNK, DN_HEADS, DN_HEAD_DIM
    NC = T // C
    W = DN_WIDTH

    @pl.when(t == 0)
    def _():
        xbuf[0:SUBLANES, :] = jnp.zeros((SUBLANES, 3 * W), F32)
        state[...] = jnp.zeros_like(state)

    @pl.when(t > 0)
    def _():
        xbuf[0:SUBLANES, :] = xbuf[T:T + SUBLANES, :]

    xbuf[SUBLANES:SUBLANES + T, 0:W] = q_ref[...].astype(F32)
    xbuf[SUBLANES:SUBLANES + T, W:2 * W] = k_ref[...].astype(F32)
    xbuf[SUBLANES:SUBLANES + T, 2 * W:3 * W] = v_ref[...].astype(F32)

    ba = ba_ref[...]
    beta_all = _sigmoid(ba)
    sp_in = ba + dtb_ref[...]
    softplus = jnp.maximum(sp_in, 0.0) + jnp.log(1.0 + jnp.exp(-jnp.abs(sp_in)))
    g_all = -jnp.exp(alog_ref[...]) * softplus

    row = lax.broadcasted_iota(jnp.int32, (T, T), 0)
    col = lax.broadcasted_iota(jnp.int32, (T, T), 1)
    same = (row // C) == (col // C)
    lower = jnp.where(jnp.logical_and(same, col <= row), 1.0, 0.0).astype(BF16)
    upper = jnp.where(jnp.logical_and(same, row <= col), 1.0, 0.0).astype(BF16)

    g1, g2, g3 = _split3(g_all)
    gc_col_all = _dot(lower, g1) + _dot(lower, g2) + _dot(lower, g3)
    g_rows = g_all.T[0:2 * H, :]
    r1, r2, r3 = _split3(g_rows)
    gc_rows = _dot(r1, upper) + _dot(r2, upper) + _dot(r3, upper)
    eg_rows = jnp.exp(gc_rows)

    wrow = lax.broadcasted_iota(jnp.int32, (C, T), 0)
    wcol = lax.broadcasted_iota(jnp.int32, (C, T), 1) % C
    causal_w = wcol <= wrow
    strict_w = wcol < wrow
    first_half = lax.broadcasted_iota(jnp.int32, (C, LANES), 1) < C

    def widen(x):
        cols = []
        for p in range(NC // 2):
            ls = slice(p * LANES, (p + 1) * LANES)
            cols.append(jnp.where(first_half, x[2 * p * C:(2 * p + 1) * C, ls],
                                  x[(2 * p + 1) * C:(2 * p + 2) * C, ls]))
        return jnp.concatenate(cols, axis=1)

    def blockdiag(xw):
        return jnp.where(same, jnp.concatenate([xw] * NC, axis=0), 0.0).astype(BF16)

    def conv(base, h):
        lo = base + h * DH
        acc = None
        for j in range(DN_CONV):
            term = xbuf[SUBLANES - (DN_CONV - 1) + j:SUBLANES - (DN_CONV - 1) + j + T, lo:lo + DH] \
                * cw_ref[j:j + 1, lo:lo + DH]
            acc = term if acc is None else acc + term
        return _silu(acc)

    def l2n(x):
        return x * lax.rsqrt(jnp.sum(x * x, axis=-1, keepdims=True) + EPS)

    def head_prep(h, done):
        q = l2n(conv(0, h)) * (DH ** -0.5)
        k = l2n(conv(W, h))
        v = conv(2 * W, h)
        beta = jnp.broadcast_to(beta_all[:, h:h + 1], (T, DH))
        gcol = jnp.broadcast_to(gc_col_all[:, H + h:H + h + 1], (T, LANES))
        egcol = jnp.exp(gcol)
        grow = gc_rows[H + h:H + h + 1, :]
        gcol_w = widen(jnp.concatenate([gcol] * (NC // 2), axis=1))
        dec_w = jnp.exp(jnp.where(causal_w, gcol_w - grow, -1e30))

        kb = k * beta
        kT = k.T
        yield
        mm = _dot(jnp.concatenate([kb, q], axis=0).astype(BF16), kT.astype(BF16))
        yield
        a_w = jnp.where(strict_w, widen(mm[:T]) * dec_w, 0.0)
        attn_w = widen(mm[T:]) * dec_w

        r_w = -a_w
        p_w = _dot(a_w.astype(BF16), blockdiag(a_w))
        yield
        n_sq = 5
        for s in range(n_sq):
            pbd = blockdiag(p_w)
            if s < n_sq - 1:
                rp = _dot(jnp.concatenate([r_w, p_w], axis=0).astype(BF16), pbd)
                r_w = r_w + p_w + rp[:C]
                p_w = rp[C:]
            else:
                r_w = r_w + p_w + _dot(r_w.astype(BF16), pbd)
            yield

        rhs = jnp.concatenate([v * beta, kb * egcol], axis=1)
        sol = rhs + _dot(blockdiag(r_w), rhs.astype(BF16))
        yield
        u0 = sol[:, :DH]
        qd = q * egcol
        wq = [jnp.concatenate([sol[c * C:(c + 1) * C, DH:], qd[c * C:(c + 1) * C]], axis=0).astype(BF16)
              for c in range(NC)]
        kd_t = kT * dec_w[C - 1:C, :]
        ak = [jnp.concatenate([attn_w[:, p * LANES:(p + 1) * LANES], kd_t[:, p * LANES:(p + 1) * LANES]],
                              axis=0).astype(BF16) for p in range(NC // 2)]
        eg = [eg_rows[H + h:H + h + 1, c * C + C - 1:c * C + C] for c in range(NC)]
        done[h] = (u0, wq, ak, eg)

    def pair_diag(x1, x2):
        z = jnp.zeros_like(x1)
        return jnp.concatenate([jnp.concatenate([x1, z], axis=1), jnp.concatenate([z, x2], axis=1)], axis=0)

    def pair_recurrence(hp, prep):
        hs = (2 * hp, 2 * hp + 1)
        pr = [prep[h] for h in hs]
        st = [state[h] for h in hs]
        outs = [[], []]
        for c in range(NC):
            res = _dot(jnp.concatenate([pr[0][1][c], pr[1][1][c]], axis=1),
                       pair_diag(st[0].astype(BF16), st[1].astype(BF16)))
            yield
            u_pairs = []
            for i in range(2):
                u = pr[i][0][c * C:(c + 1) * C] - res[:C, i * DH:(i + 1) * DH]
                zu = jnp.zeros_like(u)
                u_pairs.append(jnp.concatenate([u, zu] if c % 2 == 0 else [zu, u], axis=0).astype(BF16))
            res2 = _dot(jnp.concatenate([pr[0][2][c // 2], pr[1][2][c // 2]], axis=1),
                        pair_diag(u_pairs[0], u_pairs[1]))
            yield
            for i in range(2):
                ls = slice(i * DH, (i + 1) * DH)
                outs[i].append(res[C:, ls] + res2[:C, ls])
                st[i] = st[i] * pr[i][3][c] + res2[C:, ls]
        for i, h in enumerate(hs):
            state[h] = st[i]
            o = jnp.concatenate(outs[i], axis=0)
            o = o * lax.rsqrt(jnp.mean(o * o, axis=-1, keepdims=True) + EPS) * ng_ref[...]
            zz = z_ref[:, h * DH:(h + 1) * DH].astype(F32)
            out_ref[:, h * DH:(h + 1) * DH] = (o * _silu(zz)).astype(BF16)

    def lockstep(gens):
        gens = list(gens)
        while gens:
            gens = [g for g in gens if next(g, True) is None]

    prep = [None] * H
    lockstep(head_prep(h, prep) for h in range(H))
    lockstep(pair_recurrence(hp, prep) for hp in range(H // 2))


def _deltanet(proj, ba, conv_w, a_log, dt_bias, norm_g, bsz, seq):
    nt = seq // DN_T
    w = DN_WIDTH
    pad = lambda p: jnp.zeros((1, LANES), F32).at[0, DN_HEADS:2 * DN_HEADS].set(p)
    tok = lambda b, t: b * nt + t
    return pl.pallas_call(
        _deltanet_kernel,
        grid=(bsz, nt),
        in_specs=[pl.BlockSpec((DN_T, w), lambda b, t: (tok(b, t), 0)),
                  pl.BlockSpec((DN_T, w), lambda b, t: (tok(b, t), 1)),
                  pl.BlockSpec((DN_T, w), lambda b, t: (tok(b, t), 2)),
                  pl.BlockSpec((DN_T, w), lambda b, t: (tok(b, t), COL_Z // w)),
                  pl.BlockSpec((DN_T, LANES), lambda b, t: (tok(b, t), 0)),
                  pl.BlockSpec((DN_CONV, 3 * w), lambda b, t: (0, 0)),
                  pl.BlockSpec((1, LANES), lambda b, t: (0, 0)),
                  pl.BlockSpec((1, LANES), lambda b, t: (0, 0)),
                  pl.BlockSpec((1, DN_HEAD_DIM), lambda b, t: (0, 0))],
        out_specs=pl.BlockSpec((DN_T, w), lambda b, t: (tok(b, t), 0)),
        out_shape=jax.ShapeDtypeStruct((bsz * seq, w), BF16),
        scratch_shapes=[pltpu.VMEM((DN_T + SUBLANES, 3 * w), F32),
                        pltpu.VMEM((DN_HEADS, DN_HEAD_DIM, DN_HEAD_DIM), F32)],
        compiler_params=_params(("arbitrary", "arbitrary")),
        name="deltanet",
    )(proj, proj, proj, proj, ba, conv_w, pad(a_log), pad(dt_bias), norm_g.reshape(1, DN_HEAD_DIM))


def _conformer_kernel(val_ref, gl_ref, cw_ref, lg_ref, lb_ref, out_ref, ubuf, cbuf):
    t = pl.program_id(1)
    T = CF_T

    @pl.when(t == 0)
    def _():
        ubuf[0:CF_HALO, :] = jnp.zeros((CF_HALO, CF_WIDTH), F32)

    @pl.when(t > 0)
    def _():
        ubuf[0:CF_HALO, :] = ubuf[T:T + CF_HALO, :]

    ubuf[CF_HALO:CF_HALO + T, :] = val_ref[...].astype(F32) * _sigmoid(gl_ref[...].astype(F32))

    rows = 64
    off = CF_HALO - (CF_KERNEL - 1)
    for cb in range(CF_WIDTH // LANES):
        cs = slice(cb * LANES, (cb + 1) * LANES)
        for r0 in range(0, T, rows):
            acc = None
            for j in range(CF_KERNEL):
                term = ubuf[r0 + off + j:r0 + off + j + rows, cs] * cw_ref[j:j + 1, cs]
                acc = term if acc is None else acc + term
            cbuf[r0:r0 + rows, cs] = acc

    y = cbuf[...]
    mu = jnp.mean(y, axis=-1, keepdims=True)
    yc = y - mu
    var = jnp.mean(yc * yc, axis=-1, keepdims=True)
    yn = yc * lax.rsqrt(var + EPS) * lg_ref[...] + lb_ref[...]
    out_ref[...] = _silu(yn).astype(BF16)


def _conformer(proj, conv_w, ln_g, ln_b, bsz, seq):
    nt = seq // CF_T
    tok = lambda b, t: b * nt + t
    return pl.pallas_call(
        _conformer_kernel,
        grid=(bsz, nt),
        in_specs=[pl.BlockSpec((CF_T, CF_WIDTH), lambda b, t: (tok(b, t), COL_GLU // CF_WIDTH)),
                  pl.BlockSpec((CF_T, CF_WIDTH), lambda b, t: (tok(b, t), COL_GLU // CF_WIDTH + 1)),
                  pl.BlockSpec((CF_KERNEL, CF_WIDTH), lambda b, t: (0, 0)),
                  pl.BlockSpec((1, CF_WIDTH), lambda b, t: (0, 0)),
                  pl.BlockSpec((1, CF_WIDTH), lambda b, t: (0, 0))],
        out_specs=pl.BlockSpec((CF_T, CF_WIDTH), lambda b, t: (tok(b, t), 0)),
        out_shape=jax.ShapeDtypeStruct((bsz * seq, CF_WIDTH), BF16),
        scratch_shapes=[pltpu.VMEM((CF_T + CF_HALO, CF_WIDTH), F32),
                        pltpu.VMEM((CF_T, CF_WIDTH), F32)],
        compiler_params=_params(("arbitrary", "arbitrary")),
        name="conformer",
    )(proj, proj, conv_w, ln_g.reshape(1, CF_WIDTH), ln_b.reshape(1, CF_WIDTH))


def _merge_kernel(o_ref, u_ref, ga_ref, gb_ref, x_ref, mod_ref, g2_ref, wdo_ref, wco_ref, wout_ref,
                  x1_ref, hn2_ref):
    for r in range(0, MERGE_TM, ROW_SUB):
        rs = slice(r, r + ROW_SUB)
        branch_a = _dot(o_ref[rs, :], wdo_ref[...])
        branch_b = _dot(u_ref[rs, :], wco_ref[...])
        merged = (_sigmoid(ga_ref[rs, :].astype(F32)) * branch_a
                  + _sigmoid(gb_ref[rs, :].astype(F32)) * branch_b)
        mix = _dot(merged.astype(BF16), wout_ref[...])
        x1 = x_ref[rs, :] + mod_ref[2:3, :] * mix
        x1_ref[rs, :] = x1
        hn2 = _modulated_rmsnorm(x1, g2_ref[...], mod_ref[3:4, :], mod_ref[4:5, :])
        hn2_ref[rs, :] = hn2.astype(BF16)


def _merge(o_gated, u_act, proj, x2d, mod6, g2, w_dn_o, w_cf_o, w_out, seq):
    m, d = x2d.shape
    tiles_per_seq = seq // MERGE_TM
    const = lambda shape: pl.BlockSpec(shape, lambda i: (0, 0), pipeline_mode=pl.Buffered(1))
    return pl.pallas_call(
        _merge_kernel,
        grid=(m // MERGE_TM,),
        in_specs=[pl.BlockSpec((MERGE_TM, DN_WIDTH), lambda i: (i, 0)),
                  pl.BlockSpec((MERGE_TM, CF_WIDTH), lambda i: (i, 0)),
                  pl.BlockSpec((MERGE_TM, d), lambda i: (i, COL_GATE_A // d)),
                  pl.BlockSpec((MERGE_TM, d), lambda i: (i, COL_GATE_B // d)),
                  pl.BlockSpec((MERGE_TM, d), lambda i: (i, 0)),
                  pl.BlockSpec((None, 6, d), lambda i: (i // tiles_per_seq, 0, 0)),
                  pl.BlockSpec((1, d), lambda i: (0, 0)),
                  const((DN_WIDTH, d)),
                  const((CF_WIDTH, d)),
                  const((d, d))],
        out_specs=[pl.BlockSpec((MERGE_TM, d), lambda i: (i, 0)),
                   pl.BlockSpec((MERGE_TM, d), lambda i: (i, 0))],
        out_shape=[jax.ShapeDtypeStruct((m, d), F32),
                   jax.ShapeDtypeStruct((m, d), BF16)],
        compiler_params=_params(("arbitrary",)),
        name="merge",
    )(o_gated, u_act, proj, proj, x2d, mod6, g2, w_dn_o, w_cf_o, w_out)


def _ffn_kernel(hn_ref, x1_ref, mod_ref, gf_ref, wg_ref, wu_ref, cw_ref, wd_ref, out_ref,
                gbuf, carry, *, tiles_per_seq):
    i = pl.program_id(0)
    j = pl.program_id(1)
    nf = pl.num_programs(1)
    TM = FFN_TM

    @pl.when(j == 0)
    def _():
        out_ref[...] = jnp.zeros_like(out_ref)

    @pl.when(i == 0)
    def _():
        carry[j] = jnp.zeros((SUBLANES, FFN_TF), F32)

    keep = (i % tiles_per_seq != 0).astype(F32)
    gbuf[0:SUBLANES, :] = carry[j] * keep

    for r in range(0, TM, ROW_SUB):
        hn = hn_ref[r:r + ROW_SUB, :]
        gbuf[SUBLANES + r:SUBLANES + r + ROW_SUB, :] = _dot(hn, wg_ref[...])
        up = _dot(hn, wu_ref[...])
        acc = None
        for tap in range(FFN_CONV):
            o = SUBLANES + r - (FFN_CONV - 1) + tap
            term = gbuf[o:o + ROW_SUB, :] * cw_ref[tap:tap + 1, :]
            acc = term if acc is None else acc + term
        h = (_silu(acc) * up).astype(BF16)
        out_ref[r:r + ROW_SUB, :] += _dot(h, wd_ref[...])

    carry[j] = gbuf[TM:TM + SUBLANES, :]

    @pl.when(j == nf - 1)
    def _():
        x2 = x1_ref[...] + mod_ref[5:6, :] * out_ref[...]
        ms = jnp.mean(x2 * x2, axis=-1, keepdims=True)
        out_ref[...] = x2 * lax.rsqrt(ms + EPS) * gf_ref[...]


def _ffn(hn2, x1, mod6, gf, w_up, conv_w, w_down, seq):
    m, d = x1.shape
    nf = FFN_DIM // FFN_TF
    tiles_per_seq = seq // FFN_TM
    return pl.pallas_call(
        functools.partial(_ffn_kernel, tiles_per_seq=tiles_per_seq),
        grid=(m // FFN_TM, nf),
        in_specs=[pl.BlockSpec((FFN_TM, d), lambda i, j: (i, 0)),
                  pl.BlockSpec((FFN_TM, d), lambda i, j: (i, 0)),
                  pl.BlockSpec((None, 6, d), lambda i, j: (i // tiles_per_seq, 0, 0)),
                  pl.BlockSpec((1, d), lambda i, j: (0, 0)),
                  pl.BlockSpec((d, FFN_TF), lambda i, j: (0, j)),
                  pl.BlockSpec((d, FFN_TF), lambda i, j: (0, nf + j)),
                  pl.BlockSpec((FFN_CONV, FFN_TF), lambda i, j: (0, j)),
                  pl.BlockSpec((FFN_TF, d), lambda i, j: (j, 0))],
        out_specs=pl.BlockSpec((FFN_TM, d), lambda i, j: (i, 0)),
        out_shape=jax.ShapeDtypeStruct((m, d), F32),
        scratch_shapes=[pltpu.VMEM((FFN_TM + SUBLANES, FFN_TF), F32),
                        pltpu.VMEM((nf, SUBLANES, FFN_TF), F32)],
        compiler_params=_params(("arbitrary", "arbitrary")),
        name="ffn",
    )(hn2, x1, mod6, gf, w_up, w_up, conv_w, w_down)


def kernel(x, c, w_ada, b_ada, norm1_g, w_in, dn_conv_w, dn_a_log, dn_dt_bias, dn_norm_g, dn_w_o,
           cf_conv_w, cf_ln_g, cf_ln_b, cf_w_o, w_out, norm2_g, ffn_w_up, ffn_conv_w, ffn_w_down,
           final_norm_g):
    bsz, seq, d = x.shape
    assert w_ada.shape[0] == 1, "single-layer block"
    x2d = x.reshape(bsz * seq, d)
    for l in range(1):
        mod6 = _ada(c, w_ada[l], b_ada[l]).reshape(bsz, 6, d)
        w_main, wba = _repack_w_in(w_in[l])
        proj, ba = _inproj(x2d, mod6, norm1_g[l].reshape(1, d), w_main, wba, seq)
        o_gated = _deltanet(proj, ba, dn_conv_w[l], dn_a_log[l], dn_dt_bias[l], dn_norm_g[l], bsz, seq)
        u_act = _conformer(proj, cf_conv_w[l], cf_ln_g[l], cf_ln_b[l], bsz, seq)
        x1, hn2 = _merge(o_gated, u_act, proj, x2d, mod6, norm2_g[l].reshape(1, d),
                         dn_w_o[l].astype(BF16), cf_w_o[l].astype(BF16), w_out[l].astype(BF16), seq)
        x2d = _ffn(hn2, x1, mod6, final_norm_g.reshape(1, d), ffn_w_up[l].astype(BF16), ffn_conv_w[l],
                   ffn_w_down[l].astype(BF16), seq)
    return x2d.reshape(bsz, seq, d)
```

```python
import functools

import jax
import jax.numpy as jnp
from jax import lax
from jax.experimental import pallas as pl
from jax.experimental.pallas import tpu as pltpu

F32 = jnp.float32
BF16 = jnp.bfloat16

D_MODEL = 2048
DN_HEADS = 8
DN_HEAD_DIM = 128
DN_WIDTH = DN_HEADS * DN_HEAD_DIM
DN_CONV = 4
DN_CHUNK = 64
CF_WIDTH = 1024
CF_KERNEL = 31
FFN_DIM = 5632
FFN_CONV = 3
EPS = 1e-6

LANES = 128
SUBLANES = 8
VMEM_LIMIT = 56 * 1024 * 1024

COL_Z = 3 * DN_WIDTH
COL_GLU = COL_Z + DN_WIDTH
COL_GATE_A = COL_GLU + 2 * CF_WIDTH
COL_GATE_B = COL_GATE_A + D_MODEL
N_MAIN = COL_GATE_B + D_MODEL

ROW_SUB = 256
REPACK_ROWS = 256
ADA_TN = 1024
INPROJ_TM = 1024
INPROJ_TN = 1024
DN_T = 256
CF_T = 256
CF_HALO = 32
MERGE_TM = 512
FFN_TM = 512
FFN_TF = 512


def _dot(a, b):
    return jnp.dot(a, b, preferred_element_type=F32)


def _sigmoid(x):
    return 0.5 * jnp.tanh(0.5 * x) + 0.5


def _silu(x):
    h = 0.5 * x
    return h * jnp.tanh(h) + h


def _params(sem):
    return pltpu.CompilerParams(dimension_semantics=sem, vmem_limit_bytes=VMEM_LIMIT)


def _ada_kernel(c_ref, w_ref, b_ref, o_ref):
    c = c_ref[...]
    cact = _silu(c).astype(BF16)
    o_ref[...] = _dot(cact, w_ref[...].astype(BF16)) + b_ref[...]


def _ada(c, w, b):
    bsz, d = c.shape
    n = w.shape[1]
    return pl.pallas_call(
        _ada_kernel,
        grid=(n // ADA_TN,),
        in_specs=[pl.BlockSpec((bsz, d), lambda j: (0, 0)),
                  pl.BlockSpec((d, ADA_TN), lambda j: (0, j)),
                  pl.BlockSpec((1, ADA_TN), lambda j: (0, j))],
        out_specs=pl.BlockSpec((bsz, ADA_TN), lambda j: (0, j)),
        out_shape=jax.ShapeDtypeStruct((bsz, n), F32),
        compiler_params=_params(("arbitrary",)),
        name="ada",
    )(c, w, b.reshape(1, n))


def _repack_kernel(w_ref, main_ref, ba_ref):
    lo, hi = COL_GLU, COL_GLU + 2 * DN_HEADS
    main_ref[:, :lo] = w_ref[:, :lo].astype(BF16)
    main_ref[:, lo:] = w_ref[:, hi:].astype(BF16)
    lane = lax.broadcasted_iota(jnp.int32, (REPACK_ROWS, LANES), 1)
    ba_ref[...] = jnp.where(lane < 2 * DN_HEADS, w_ref[:, lo:lo + LANES], 0.0).astype(BF16)


def _repack_w_in(w, l):
    _, d, n = w.shape
    return pl.pallas_call(
        _repack_kernel,
        grid=(d // REPACK_ROWS,),
        in_specs=[pl.BlockSpec((None, REPACK_ROWS, n), lambda i: (l, i, 0))],
        out_specs=[pl.BlockSpec((REPACK_ROWS, N_MAIN), lambda i: (i, 0)),
                   pl.BlockSpec((REPACK_ROWS, LANES), lambda i: (i, 0))],
        out_shape=[jax.ShapeDtypeStruct((d, N_MAIN), BF16),
                   jax.ShapeDtypeStruct((d, LANES), BF16)],
        compiler_params=_params(("arbitrary",)),
        name="repack",
    )(w)


def _modulated_rmsnorm(x, g, shift, scale):
    ms = jnp.mean(x * x, axis=-1, keepdims=True)
    y = x * lax.rsqrt(ms + EPS) * g
    return y * (1.0 + scale) + shift


def _inproj_kernel(x_ref, mod_ref, g_ref, w_ref, wba_ref, cw_ref, out_ref, ba_ref, hn_ref, cbuf, obuf, carry,
                   *, tiles_per_seq):
    i = pl.program_id(0)
    j = pl.program_id(1)
    TM = INPROJ_TM

    @pl.when(j == 0)
    def _():
        for r in range(0, TM, ROW_SUB):
            hn = _modulated_rmsnorm(x_ref[r:r + ROW_SUB, :], g_ref[...], mod_ref[0:1, :], mod_ref[1:2, :])
            hn_ref[r:r + ROW_SUB, :] = hn.astype(BF16)
        ba_ref[...] = _dot(hn_ref[...], wba_ref[...])

    @pl.when(jnp.logical_and(i == 0, j == 0))
    def _():
        carry[...] = jnp.zeros_like(carry)

    def conv_tile(slot, l2_scale):
        keep = (i % tiles_per_seq != 0).astype(F32)
        cbuf[:, 0:SUBLANES, :] = carry[slot] * keep
        n = ROW_SUB // SUBLANES
        for r in range(0, TM, ROW_SUB):
            res = _dot(hn_ref[r:r + ROW_SUB, :], w_ref[...])
            for cb in range(INPROJ_TN // LANES):
                cs = slice(cb * LANES, (cb + 1) * LANES)
                col = cbuf.at[cb]
                col[SUBLANES + r:SUBLANES + r + ROW_SUB, :] = res[:, cs]
                base = SUBLANES + r - (DN_CONV - 1)
                acc = [None] * SUBLANES
                for s in range(SUBLANES - 1 + DN_CONV):
                    xs = col[pl.ds(base + s, n, stride=SUBLANES), :]
                    for p in range(SUBLANES):
                        tap = s - p
                        if 0 <= tap < DN_CONV:
                            term = xs * cw_ref[tap:tap + 1, cs]
                            acc[p] = term if acc[p] is None else acc[p] + term
                for p in range(SUBLANES):
                    y = _silu(acc[p])
                    if l2_scale is not None:
                        y = y * (lax.rsqrt(jnp.sum(y * y, axis=-1, keepdims=True) + EPS) * l2_scale)
                    obuf.at[cb][pl.ds(r + p, n, stride=SUBLANES), :] = y
                out_ref[r:r + ROW_SUB, cs] = obuf[cb, r:r + ROW_SUB, :].astype(BF16)
        carry[slot] = cbuf[:, TM:TM + SUBLANES, :]

    pl.when(j == 0)(functools.partial(conv_tile, 0, DN_HEAD_DIM ** -0.5))
    pl.when(j == 1)(functools.partial(conv_tile, 1, 1.0))
    pl.when(j == 2)(functools.partial(conv_tile, 2, None))

    @pl.when(j >= 3)
    def _():
        for r in range(0, TM, ROW_SUB):
            out_ref[r:r + ROW_SUB, :] = _dot(hn_ref[r:r + ROW_SUB, :], w_ref[...]).astype(BF16)


def _inproj(x2d, mod6, g1, w_main, wba, conv_w, seq):
    m, d = x2d.shape
    n = w_main.shape[1]
    assert INPROJ_TN == DN_WIDTH and DN_HEAD_DIM == LANES, "q / k / v: one column tile each, one head per lane tile"
    tiles_per_seq = seq // INPROJ_TM
    n_conv = 3
    nb = INPROJ_TN // LANES
    return pl.pallas_call(
        functools.partial(_inproj_kernel, tiles_per_seq=tiles_per_seq),
        grid=(m // INPROJ_TM, n // INPROJ_TN),
        in_specs=[pl.BlockSpec((INPROJ_TM, d), lambda i, j: (i, 0)),
                  pl.BlockSpec((None, 6, d), lambda i, j: (i // tiles_per_seq, 0, 0)),
                  pl.BlockSpec((1, d), lambda i, j: (0, 0)),
                  pl.BlockSpec((d, INPROJ_TN), lambda i, j: (0, j)),
                  pl.BlockSpec((d, LANES), lambda i, j: (0, 0)),
                  pl.BlockSpec((DN_CONV, INPROJ_TN), lambda i, j: (0, jnp.minimum(j, n_conv - 1)))],
        out_specs=[pl.BlockSpec((INPROJ_TM, INPROJ_TN), lambda i, j: (i, j)),
                   pl.BlockSpec((INPROJ_TM, LANES), lambda i, j: (i, 0))],
        out_shape=[jax.ShapeDtypeStruct((m, n), BF16),
                   jax.ShapeDtypeStruct((m, LANES), F32)],
        scratch_shapes=[pltpu.VMEM((INPROJ_TM, d), BF16),
                        pltpu.VMEM((nb, INPROJ_TM + SUBLANES, LANES), F32),
                        pltpu.VMEM((nb, INPROJ_TM, LANES), F32),
                        pltpu.VMEM((n_conv, nb, SUBLANES, LANES), F32)],
        compiler_params=_params(("arbitrary", "arbitrary")),
        name="inproj",
    )(x2d, mod6, g1, w_main, wba, conv_w)


def _split3(x):
    h1 = x.astype(BF16)
    r1 = x - h1.astype(F32)
    h2 = r1.astype(BF16)
    h3 = (r1 - h2.astype(F32)).astype(BF16)
    return h1, h2, h3


def _deltanet_kernel(q_ref, k_ref, v_ref, z_ref, ba_ref, alog_ref, dtb_ref, ng_ref, out_ref, state):
    t = pl.program_id(1)
    T, C, H, DH = DN_T, DN_CHUNK, DN_HEADS, DN_HEAD_DIM
    NC = T // C

    @pl.when(t == 0)
    def _():
        state[...] = jnp.zeros_like(state)

    ba = ba_ref[...]
    beta_all = _sigmoid(ba)
    sp_in = ba + dtb_ref[...]
    softplus = jnp.maximum(sp_in, 0.0) + jnp.log(1.0 + jnp.exp(-jnp.abs(sp_in)))
    g_all = -jnp.exp(alog_ref[...]) * softplus

    row = lax.broadcasted_iota(jnp.int32, (T, T), 0)
    col = lax.broadcasted_iota(jnp.int32, (T, T), 1)
    same = (row // C) == (col // C)
    lower = jnp.where(jnp.logical_and(same, col <= row), 1.0, 0.0).astype(BF16)
    upper = jnp.where(jnp.logical_and(same, row <= col), 1.0, 0.0).astype(BF16)

    g1, g2, g3 = _split3(g_all)
    gc_col_all = _dot(lower, g1) + _dot(lower, g2) + _dot(lower, g3)
    g_rows = g_all.T[0:2 * H, :]
    r1, r2, r3 = _split3(g_rows)
    gc_rows = _dot(r1, upper) + _dot(r2, upper) + _dot(r3, upper)
    eg_rows = jnp.exp(gc_rows)

    wrow = lax.broadcasted_iota(jnp.int32, (C, T), 0)
    wcol = lax.broadcasted_iota(jnp.int32, (C, T), 1) % C
    causal_w = wcol <= wrow
    strict_w = wcol < wrow
    first_half = lax.broadcasted_iota(jnp.int32, (C, LANES), 1) < C

    def widen(x):
        cols = []
        for p in range(NC // 2):
            ls = slice(p * LANES, (p + 1) * LANES)
            cols.append(jnp.where(first_half, x[2 * p * C:(2 * p + 1) * C, ls],
                                  x[(2 * p + 1) * C:(2 * p + 2) * C, ls]))
        return jnp.concatenate(cols, axis=1)

    same_b = jnp.where(same, 1.0, 0.0).astype(BF16)

    def blockdiag(xw):
        return jnp.concatenate([xw.astype(BF16)] * NC, axis=0) * same_b

    def head_prep(h, done):
        hs = slice(h * DH, (h + 1) * DH)
        qb = q_ref[:, hs]
        q = qb.astype(F32)
        k = k_ref[:, hs].astype(F32)
        v = v_ref[:, hs].astype(F32)
        beta = jnp.broadcast_to(beta_all[:, h:h + 1], (T, DH))
        gcol = jnp.broadcast_to(gc_col_all[:, H + h:H + h + 1], (T, LANES))
        egcol = jnp.exp(gcol)
        grow = gc_rows[H + h:H + h + 1, :]
        gcol_w = widen(jnp.concatenate([gcol] * (NC // 2), axis=1))
        dec_w = jnp.exp(jnp.where(causal_w, gcol_w - grow, -1e30))

        kb = k * beta
        kT = k.T
        yield
        mm = _dot(jnp.concatenate([kb.astype(BF16), qb], axis=0), kT.astype(BF16))
        yield
        a_w = jnp.where(strict_w, widen(mm[:T]) * dec_w, 0.0)
        attn_w = widen(mm[T:]) * dec_w

        r_w = -a_w
        p_w = _dot(a_w.astype(BF16), blockdiag(a_w))
        yield
        n_sq = 5
        for s in range(n_sq):
            pbd = blockdiag(p_w)
            if s < n_sq - 1:
                rp = _dot(jnp.concatenate([r_w, p_w], axis=0).astype(BF16), pbd)
                r_w = r_w + p_w + rp[:C]
                p_w = rp[C:]
            else:
                r_w = r_w + p_w + _dot(r_w.astype(BF16), pbd)
            yield

        rhs = jnp.concatenate([v * beta, kb * egcol], axis=1)
        sol = rhs + _dot(blockdiag(r_w), rhs.astype(BF16))
        yield
        u0 = sol[:, :DH]
        qd = q * egcol
        wq = [jnp.concatenate([sol[c * C:(c + 1) * C, DH:], qd[c * C:(c + 1) * C]], axis=0).astype(BF16)
              for c in range(NC)]
        kd_t = kT * dec_w[C - 1:C, :]
        ak = [jnp.concatenate([attn_w[:, p * LANES:(p + 1) * LANES], kd_t[:, p * LANES:(p + 1) * LANES]],
                              axis=0).astype(BF16) for p in range(NC // 2)]
        eg = [eg_rows[H + h:H + h + 1, c * C + C - 1:c * C + C] for c in range(NC)]
        done[h] = (u0, wq, ak, eg)

    def pair_diag(x1, x2):
        z = jnp.zeros_like(x1)
        return jnp.concatenate([jnp.concatenate([x1, z], axis=1), jnp.concatenate([z, x2], axis=1)], axis=0)

    def pair_recurrence(hp, prep):
        hs = (2 * hp, 2 * hp + 1)
        pr = [prep[h] for h in hs]
        st = [state[h] for h in hs]
        outs = [[], []]
        for c in range(NC):
            res = _dot(jnp.concatenate([pr[0][1][c], pr[1][1][c]], axis=1),
                       pair_diag(st[0].astype(BF16), st[1].astype(BF16)))
            yield
            u_pairs = []
            for i in range(2):
                u = pr[i][0][c * C:(c + 1) * C] - res[:C, i * DH:(i + 1) * DH]
                zu = jnp.zeros_like(u)
                u_pairs.append(jnp.concatenate([u, zu] if c % 2 == 0 else [zu, u], axis=0).astype(BF16))
            res2 = _dot(jnp.concatenate([pr[0][2][c // 2], pr[1][2][c // 2]], axis=1),
                        pair_diag(u_pairs[0], u_pairs[1]))
            yield
            for i in range(2):
                ls = slice(i * DH, (i + 1) * DH)
                outs[i].append(res[C:, ls] + res2[:C, ls])
                st[i] = st[i] * pr[i][3][c] + res2[C:, ls]
        for i, h in enumerate(hs):
            state[h] = st[i]
            o = jnp.concatenate(outs[i], axis=0)
            o = o * lax.rsqrt(jnp.mean(o * o, axis=-1, keepdims=True) + EPS) * ng_ref[...]
            zz = z_ref[:, h * DH:(h + 1) * DH].astype(F32)
            out_ref[:, h * DH:(h + 1) * DH] = (o * _silu(zz)).astype(BF16)

    def lockstep(gens):
        gens = list(gens)
        while gens:
            gens = [g for g in gens if next(g, True) is None]

    prep = [None] * H
    lockstep(head_prep(h, prep) for h in range(H))
    lockstep(pair_recurrence(hp, prep) for hp in range(H // 2))


def _deltanet(proj, ba, a_log, dt_bias, norm_g, bsz, seq):
    nt = seq // DN_T
    w = DN_WIDTH
    pad = lambda p: jnp.zeros((1, LANES), F32).at[0, DN_HEADS:2 * DN_HEADS].set(p)
    tok = lambda b, t: b * nt + t
    return pl.pallas_call(
        _deltanet_kernel,
        grid=(bsz, nt),
        in_specs=[pl.BlockSpec((DN_T, w), lambda b, t: (tok(b, t), 0)),
                  pl.BlockSpec((DN_T, w), lambda b, t: (tok(b, t), 1)),
                  pl.BlockSpec((DN_T, w), lambda b, t: (tok(b, t), 2)),
                  pl.BlockSpec((DN_T, w), lambda b, t: (tok(b, t), COL_Z // w)),
                  pl.BlockSpec((DN_T, LANES), lambda b, t: (tok(b, t), 0)),
                  pl.BlockSpec((1, LANES), lambda b, t: (0, 0)),
                  pl.BlockSpec((1, LANES), lambda b, t: (0, 0)),
                  pl.BlockSpec((1, DN_HEAD_DIM), lambda b, t: (0, 0))],
        out_specs=pl.BlockSpec((DN_T, w), lambda b, t: (tok(b, t), 0)),
        out_shape=jax.ShapeDtypeStruct((bsz * seq, w), BF16),
        scratch_shapes=[pltpu.VMEM((DN_HEADS, DN_HEAD_DIM, DN_HEAD_DIM), F32)],
        compiler_params=_params(("arbitrary", "arbitrary")),
        name="deltanet",
    )(proj, proj, proj, proj, ba, pad(a_log), pad(dt_bias), norm_g.reshape(1, DN_HEAD_DIM))


def _conformer_kernel(val_ref, gl_ref, cw_ref, lg_ref, lb_ref, out_ref, ubuf, cbuf):
    t = pl.program_id(1)
    T = CF_T
    NB = CF_WIDTH // LANES

    @pl.when(t == 0)
    def _():
        ubuf[:, 0:CF_HALO, :] = jnp.zeros((NB, CF_HALO, LANES), F32)

    @pl.when(t > 0)
    def _():
        ubuf[:, 0:CF_HALO, :] = ubuf[:, T:T + CF_HALO, :]

    off = CF_HALO - (CF_KERNEL - 1)
    n = T // SUBLANES
    for cb in range(NB):
        cs = slice(cb * LANES, (cb + 1) * LANES)
        ucol = ubuf.at[cb]
        ucol[CF_HALO:CF_HALO + T, :] = val_ref[:, cs].astype(F32) * _sigmoid(gl_ref[:, cs].astype(F32))
        acc = [None] * SUBLANES
        for s in range(off, off + SUBLANES - 1 + CF_KERNEL):
            xs = ucol[pl.ds(s, n, stride=SUBLANES), :]
            for p in range(SUBLANES):
                j = s - off - p
                if 0 <= j < CF_KERNEL:
                    term = xs * cw_ref[j:j + 1, cs]
                    acc[p] = term if acc[p] is None else acc[p] + term
        for p in range(SUBLANES):
            cbuf.at[cb][pl.ds(p, n, stride=SUBLANES), :] = acc[p]

    y = jnp.concatenate([cbuf[cb] for cb in range(NB)], axis=1)
    mu = jnp.mean(y, axis=-1, keepdims=True)
    yc = y - mu
    var = jnp.mean(yc * yc, axis=-1, keepdims=True)
    yn = yc * lax.rsqrt(var + EPS) * lg_ref[...] + lb_ref[...]
    out_ref[...] = _silu(yn).astype(BF16)


def _conformer(proj, conv_w, ln_g, ln_b, bsz, seq):
    nt = seq // CF_T
    tok = lambda b, t: b * nt + t
    return pl.pallas_call(
        _conformer_kernel,
        grid=(bsz, nt),
        in_specs=[pl.BlockSpec((CF_T, CF_WIDTH), lambda b, t: (tok(b, t), COL_GLU // CF_WIDTH)),
                  pl.BlockSpec((CF_T, CF_WIDTH), lambda b, t: (tok(b, t), COL_GLU // CF_WIDTH + 1)),
                  pl.BlockSpec((CF_KERNEL, CF_WIDTH), lambda b, t: (0, 0)),
                  pl.BlockSpec((1, CF_WIDTH), lambda b, t: (0, 0)),
                  pl.BlockSpec((1, CF_WIDTH), lambda b, t: (0, 0))],
        out_specs=pl.BlockSpec((CF_T, CF_WIDTH), lambda b, t: (tok(b, t), 0)),
        out_shape=jax.ShapeDtypeStruct((bsz * seq, CF_WIDTH), BF16),
        scratch_shapes=[pltpu.VMEM((CF_WIDTH // LANES, CF_T + CF_HALO, LANES), F32),
                        pltpu.VMEM((CF_WIDTH // LANES, CF_T, LANES), F32)],
        compiler_params=_params(("arbitrary", "arbitrary")),
        name="conformer",
    )(proj, proj, conv_w, ln_g.reshape(1, CF_WIDTH), ln_b.reshape(1, CF_WIDTH))


def _merge_kernel(o_ref, u_ref, ga_ref, gb_ref, x_ref, mod_ref, g2_ref, wdo_ref, wco_ref, wout_ref,
                  x1_ref, hn2_ref):
    for r in range(0, MERGE_TM, ROW_SUB):
        rs = slice(r, r + ROW_SUB)
        branch_a = _dot(o_ref[rs, :], wdo_ref[...])
        branch_b = _dot(u_ref[rs, :], wco_ref[...])
        merged = (_sigmoid(ga_ref[rs, :].astype(F32)) * branch_a
                  + _sigmoid(gb_ref[rs, :].astype(F32)) * branch_b)
        mix = _dot(merged.astype(BF16), wout_ref[...])
        x1 = x_ref[rs, :] + mod_ref[2:3, :] * mix
        x1_ref[rs, :] = x1
        hn2 = _modulated_rmsnorm(x1, g2_ref[...], mod_ref[3:4, :], mod_ref[4:5, :])
        hn2_ref[rs, :] = hn2.astype(BF16)


def _merge(o_gated, u_act, proj, x2d, mod6, g2, w_dn_o, w_cf_o, w_out, seq):
    m, d = x2d.shape
    tiles_per_seq = seq // MERGE_TM
    const = lambda shape: pl.BlockSpec(shape, lambda i: (0, 0), pipeline_mode=pl.Buffered(1))
    return pl.pallas_call(
        _merge_kernel,
        grid=(m // MERGE_TM,),
        in_specs=[pl.BlockSpec((MERGE_TM, DN_WIDTH), lambda i: (i, 0)),
                  pl.BlockSpec((MERGE_TM, CF_WIDTH), lambda i: (i, 0)),
                  pl.BlockSpec((MERGE_TM, d), lambda i: (i, COL_GATE_A // d)),
                  pl.BlockSpec((MERGE_TM, d), lambda i: (i, COL_GATE_B // d)),
                  pl.BlockSpec((MERGE_TM, d), lambda i: (i, 0)),
                  pl.BlockSpec((None, 6, d), lambda i: (i // tiles_per_seq, 0, 0)),
                  pl.BlockSpec((1, d), lambda i: (0, 0)),
                  const((DN_WIDTH, d)),
                  const((CF_WIDTH, d)),
                  const((d, d))],
        out_specs=[pl.BlockSpec((MERGE_TM, d), lambda i: (i, 0)),
                   pl.BlockSpec((MERGE_TM, d), lambda i: (i, 0))],
        out_shape=[jax.ShapeDtypeStruct((m, d), F32),
                   jax.ShapeDtypeStruct((m, d), BF16)],
        compiler_params=_params(("arbitrary",)),
        name="merge",
    )(o_gated, u_act, proj, proj, x2d, mod6, g2, w_dn_o, w_cf_o, w_out)


def _ffn_kernel(hn_ref, x1_ref, mod_ref, gf_ref, wg_ref, wu_ref, cw_ref, wd_ref, out_ref,
                gbuf, carry, *, tiles_per_seq):
    i = pl.program_id(0)
    j = pl.program_id(1)
    nf = pl.num_programs(1)
    TM = FFN_TM

    @pl.when(j == 0)
    def _():
        out_ref[...] = jnp.zeros_like(out_ref)

    @pl.when(i == 0)
    def _():
        carry[j] = jnp.zeros((SUBLANES, FFN_TF), F32)

    keep = (i % tiles_per_seq != 0).astype(F32)
    gbuf[0:SUBLANES, :] = carry[j] * keep

    for r in range(0, TM, ROW_SUB):
        hn = hn_ref[r:r + ROW_SUB, :]
        gbuf[SUBLANES + r:SUBLANES + r + ROW_SUB, :] = _dot(hn, wg_ref[...])
        up = _dot(hn, wu_ref[...])
        acc = None
        for tap in range(FFN_CONV):
            o = SUBLANES + r - (FFN_CONV - 1) + tap
            term = gbuf[o:o + ROW_SUB, :] * cw_ref[tap:tap + 1, :]
            acc = term if acc is None else acc + term
        h = (_silu(acc) * up).astype(BF16)
        out_ref[r:r + ROW_SUB, :] += _dot(h, wd_ref[...])

    carry[j] = gbuf[TM:TM + SUBLANES, :]

    @pl.when(j == nf - 1)
    def _():
        x2 = x1_ref[...] + mod_ref[5:6, :] * out_ref[...]
        ms = jnp.mean(x2 * x2, axis=-1, keepdims=True)
        out_ref[...] = x2 * lax.rsqrt(ms + EPS) * gf_ref[...]


def _ffn(hn2, x1, mod6, gf, w_up, conv_w, w_down, seq):
    m, d = x1.shape
    nf = FFN_DIM // FFN_TF
    tiles_per_seq = seq // FFN_TM
    return pl.pallas_call(
        functools.partial(_ffn_kernel, tiles_per_seq=tiles_per_seq),
        grid=(m // FFN_TM, nf),
        in_specs=[pl.BlockSpec((FFN_TM, d), lambda i, j: (i, 0)),
                  pl.BlockSpec((FFN_TM, d), lambda i, j: (i, 0)),
                  pl.BlockSpec((None, 6, d), lambda i, j: (i // tiles_per_seq, 0, 0)),
                  pl.BlockSpec((1, d), lambda i, j: (0, 0)),
                  pl.BlockSpec((d, FFN_TF), lambda i, j: (0, j)),
                  pl.BlockSpec((d, FFN_TF), lambda i, j: (0, nf + j)),
                  pl.BlockSpec((FFN_CONV, FFN_TF), lambda i, j: (0, j)),
                  pl.BlockSpec((FFN_TF, d), lambda i, j: (j, 0))],
        out_specs=pl.BlockSpec((FFN_TM, d), lambda i, j: (i, 0)),
        out_shape=jax.ShapeDtypeStruct((m, d), F32),
        scratch_shapes=[pltpu.VMEM((FFN_TM + SUBLANES, FFN_TF), F32),
                        pltpu.VMEM((nf, SUBLANES, FFN_TF), F32)],
        compiler_params=_params(("arbitrary", "arbitrary")),
        name="ffn",
    )(hn2, x1, mod6, gf, w_up, w_up, conv_w, w_down)


def kernel(x, c, w_ada, b_ada, norm1_g, w_in, dn_conv_w, dn_a_log, dn_dt_bias, dn_norm_g, dn_w_o,
           cf_conv_w, cf_ln_g, cf_ln_b, cf_w_o, w_out, norm2_g, ffn_w_up, ffn_conv_w, ffn_w_down,
           final_norm_g):
    bsz, seq, d = x.shape
    assert w_ada.shape[0] == 1, "single-layer block"
    x2d = x.reshape(bsz * seq, d)
    for l in range(1):
        mod6 = _ada(c, w_ada[l], b_ada[l]).reshape(bsz, 6, d)
        w_main, wba = _repack_w_in(w_in, l)
        proj, ba = _inproj(x2d, mod6, norm1_g[l].reshape(1, d), w_main, wba, dn_conv_w[l], seq)
        o_gated = _deltanet(proj, ba, dn_a_log[l], dn_dt_bias[l], dn_norm_g[l], bsz, seq)
        u_act = _conformer(proj, cf_conv_w[l], cf_ln_g[l], cf_ln_b[l], bsz, seq)
        x1, hn2 = _merge(o_gated, u_act, proj, x2d, mod6, norm2_g[l].reshape(1, d),
                         dn_w_o[l].astype(BF16), cf_w_o[l].astype(BF16), w_out[l].astype(BF16), seq)
        x2d = _ffn(hn2, x1, mod6, final_norm_g.reshape(1, d), ffn_w_up[l].astype(BF16), ffn_conv_w[l],
                   ffn_w_down[l].astype(BF16), seq)
    return x2d.reshape(bsz, seq, d)
```

```python
import functools

import jax
import jax.numpy as jnp
from jax import lax
from jax.experimental import pallas as pl
from jax.experimental.pallas import tpu as pltpu

F32 = jnp.float32
BF16 = jnp.bfloat16

D_MODEL = 2048
DN_HEADS = 8
DN_HEAD_DIM = 128
DN_WIDTH = DN_HEADS * DN_HEAD_DIM
DN_CONV = 4
DN_CHUNK = 64
CF_WIDTH = 1024
CF_KERNEL = 31
FFN_DIM = 5632
FFN_CONV = 3
EPS = 1e-6

LANES = 128
SUBLANES = 8
VMEM_LIMIT = 60000 * 1024

COL_Z = 3 * DN_WIDTH
COL_GLU = COL_Z + DN_WIDTH
COL_GATE_A = COL_GLU + 2 * CF_WIDTH
COL_GATE_B = COL_GATE_A + D_MODEL
N_MAIN = COL_GATE_B + D_MODEL

ROW_SUB = 256
REPACK_COLS = 512
ADA_TN = 1024
INPROJ_TM = 1024
INPROJ_TN = 1024
DN_T = 256
CF_T = 256
CF_HALO = 32
MERGE_TM = 512
FFN_TM = 512
FFN_TF = 512


def _dot(a, b):
    return jnp.dot(a, b, preferred_element_type=F32)


def _sigmoid(x):
    return 0.5 * jnp.tanh(0.5 * x) + 0.5


def _silu(x):
    h = 0.5 * x
    return h * jnp.tanh(h) + h


def _params(sem):
    return pltpu.CompilerParams(dimension_semantics=sem, vmem_limit_bytes=VMEM_LIMIT)


def _ada_kernel(c_ref, w_ref, b_ref, o_ref):
    c = c_ref[...]
    cact = _silu(c).astype(BF16)
    o_ref[...] = _dot(cact, w_ref[...].astype(BF16)) + b_ref[...]


def _ada(c, w, b):
    bsz, d = c.shape
    n = w.shape[1]
    return pl.pallas_call(
        _ada_kernel,
        grid=(n // ADA_TN,),
        in_specs=[pl.BlockSpec((bsz, d), lambda j: (0, 0)),
                  pl.BlockSpec((d, ADA_TN), lambda j: (0, j)),
                  pl.BlockSpec((1, ADA_TN), lambda j: (0, j))],
        out_specs=pl.BlockSpec((bsz, ADA_TN), lambda j: (0, j)),
        out_shape=jax.ShapeDtypeStruct((bsz, n), F32),
        compiler_params=_params(("arbitrary",)),
        name="ada",
    )(c, w, b.reshape(1, n))


def _repack_kernel(wt_ref, main_ref):
    main_ref[...] = wt_ref[...].T.astype(BF16)


def _repack_w_in(w_t):
    n, d = w_t.shape
    lo, skip = COL_GLU, 2 * DN_HEADS
    assert lo % REPACK_COLS == 0 and skip % SUBLANES == 0
    return pl.pallas_call(
        _repack_kernel,
        grid=(N_MAIN // REPACK_COLS,),
        in_specs=[pl.BlockSpec((pl.Element(REPACK_COLS), pl.Element(d)),
                               lambda c: ((c * (REPACK_COLS // SUBLANES)
                                           + jnp.where(c >= lo // REPACK_COLS, skip // SUBLANES, 0)) * SUBLANES, 0))],
        out_specs=pl.BlockSpec((d, REPACK_COLS), lambda c: (0, c)),
        out_shape=jax.ShapeDtypeStruct((d, N_MAIN), BF16),
        compiler_params=_params(("arbitrary",)),
        name="repack",
    )(w_t)


def _modulated_rmsnorm(x, g, shift, scale):
    ms = jnp.mean(x * x, axis=-1, keepdims=True)
    y = x * lax.rsqrt(ms + EPS) * g
    return y * (1.0 + scale) + shift


def _inproj_kernel(x_ref, mod_ref, g_ref, w_ref, wba_ref, cw_ref, out_ref, ba_ref, hn_ref, cbuf, obuf, carry,
                   *, tiles_per_seq):
    i = pl.program_id(0)
    j = pl.program_id(1)
    TM = INPROJ_TM

    @pl.when(j == 0)
    def _():
        for r in range(0, TM, ROW_SUB):
            hn = _modulated_rmsnorm(x_ref[r:r + ROW_SUB, :], g_ref[...], mod_ref[0:1, :], mod_ref[1:2, :])
            hn_ref[r:r + ROW_SUB, :] = hn.astype(BF16)
        ba_ref[...] = _dot(hn_ref[...], wba_ref[...])

    @pl.when(jnp.logical_and(i == 0, j == 0))
    def _():
        carry[...] = jnp.zeros_like(carry)

    def conv_tile(slot, l2_scale):
        keep = (i % tiles_per_seq != 0).astype(F32)
        cbuf[:, 0:SUBLANES, :] = carry[slot] * keep
        n = ROW_SUB // SUBLANES
        for r in range(0, TM, ROW_SUB):
            res = _dot(hn_ref[r:r + ROW_SUB, :], w_ref[...])
            for cb in range(INPROJ_TN // LANES):
                cs = slice(cb * LANES, (cb + 1) * LANES)
                col = cbuf.at[cb]
                col[SUBLANES + r:SUBLANES + r + ROW_SUB, :] = res[:, cs]
                base = SUBLANES + r - (DN_CONV - 1)
                acc = [None] * SUBLANES
                for s in range(SUBLANES - 1 + DN_CONV):
                    xs = col[pl.ds(base + s, n, stride=SUBLANES), :]
                    for p in range(SUBLANES):
                        tap = s - p
                        if 0 <= tap < DN_CONV:
                            term = xs * cw_ref[tap:tap + 1, cs]
                            acc[p] = term if acc[p] is None else acc[p] + term
                for p in range(SUBLANES):
                    y = _silu(acc[p])
                    if l2_scale is not None:
                        y = y * (lax.rsqrt(jnp.sum(y * y, axis=-1, keepdims=True) + EPS) * l2_scale)
                    obuf.at[cb][pl.ds(r + p, n, stride=SUBLANES), :] = y
                out_ref[r:r + ROW_SUB, cs] = obuf[cb, r:r + ROW_SUB, :].astype(BF16)
        carry[slot] = cbuf[:, TM:TM + SUBLANES, :]

    pl.when(j == 0)(functools.partial(conv_tile, 0, DN_HEAD_DIM ** -0.5))
    pl.when(j == 1)(functools.partial(conv_tile, 1, 1.0))
    pl.when(j == 2)(functools.partial(conv_tile, 2, None))

    @pl.when(j >= 3)
    def _():
        for r in range(0, TM, ROW_SUB):
            out_ref[r:r + ROW_SUB, :] = _dot(hn_ref[r:r + ROW_SUB, :], w_ref[...]).astype(BF16)


def _inproj(x2d, mod6, g1, w_main, wba, conv_w, seq):
    m, d = x2d.shape
    n = w_main.shape[1]
    assert INPROJ_TN == DN_WIDTH and DN_HEAD_DIM == LANES, "q / k / v: one column tile each, one head per lane tile"
    tiles_per_seq = seq // INPROJ_TM
    n_conv = 3
    nb = INPROJ_TN // LANES
    return pl.pallas_call(
        functools.partial(_inproj_kernel, tiles_per_seq=tiles_per_seq),
        grid=(m // INPROJ_TM, n // INPROJ_TN),
        in_specs=[pl.BlockSpec((INPROJ_TM, d), lambda i, j: (i, 0)),
                  pl.BlockSpec((None, 6, d), lambda i, j: (i // tiles_per_seq, 0, 0)),
                  pl.BlockSpec((1, d), lambda i, j: (0, 0)),
                  pl.BlockSpec((d, INPROJ_TN), lambda i, j: (0, j)),
                  pl.BlockSpec((d, LANES), lambda i, j: (0, 0)),
                  pl.BlockSpec((DN_CONV, INPROJ_TN), lambda i, j: (0, jnp.minimum(j, n_conv - 1)))],
        out_specs=[pl.BlockSpec((INPROJ_TM, INPROJ_TN), lambda i, j: (i, j)),
                   pl.BlockSpec((INPROJ_TM, LANES), lambda i, j: (i, 0))],
        out_shape=[jax.ShapeDtypeStruct((m, n), BF16),
                   jax.ShapeDtypeStruct((m, LANES), F32)],
        scratch_shapes=[pltpu.VMEM((INPROJ_TM, d), BF16),
                        pltpu.VMEM((nb, INPROJ_TM + SUBLANES, LANES), F32),
                        pltpu.VMEM((nb, INPROJ_TM, LANES), F32),
                        pltpu.VMEM((n_conv, nb, SUBLANES, LANES), F32)],
        compiler_params=_params(("arbitrary", "arbitrary")),
        name="inproj",
    )(x2d, mod6, g1, w_main, wba, conv_w)


def _split3(x):
    h1 = x.astype(BF16)
    r1 = x - h1.astype(F32)
    h2 = r1.astype(BF16)
    h3 = (r1 - h2.astype(F32)).astype(BF16)
    return h1, h2, h3


def _deltanet_kernel(q_ref, k_ref, v_ref, z_ref, ba_ref, alog_ref, dtb_ref, ng_ref, out_ref, state):
    t = pl.program_id(1)
    T, C, H, DH = DN_T, DN_CHUNK, DN_HEADS, DN_HEAD_DIM
    NC = T // C

    @pl.when(t == 0)
    def _():
        state[...] = jnp.zeros_like(state)

    ba = ba_ref[...]
    beta_all = _sigmoid(ba)
    sp_in = ba + dtb_ref[...]
    softplus = jnp.maximum(sp_in, 0.0) + jnp.log(1.0 + jnp.exp(-jnp.abs(sp_in)))
    g_all = -jnp.exp(alog_ref[...]) * softplus

    row = lax.broadcasted_iota(jnp.int32, (T, T), 0)
    col = lax.broadcasted_iota(jnp.int32, (T, T), 1)
    same = (row // C) == (col // C)
    lower = jnp.where(jnp.logical_and(same, col <= row), 1.0, 0.0).astype(BF16)
    upper = jnp.where(jnp.logical_and(same, row <= col), 1.0, 0.0).astype(BF16)

    g1, g2, g3 = _split3(g_all)
    gc_col_all = _dot(lower, g1) + _dot(lower, g2) + _dot(lower, g3)
    g_rows = g_all.T[0:2 * H, :]
    r1, r2, r3 = _split3(g_rows)
    gc_rows = _dot(r1, upper) + _dot(r2, upper) + _dot(r3, upper)
    eg_rows = jnp.exp(gc_rows)

    wrow = lax.broadcasted_iota(jnp.int32, (C, T), 0)
    wcol = lax.broadcasted_iota(jnp.int32, (C, T), 1) % C
    causal_w = wcol <= wrow
    strict_w = wcol < wrow
    first_half = lax.broadcasted_iota(jnp.int32, (C, LANES), 1) < C

    def widen(x):
        cols = []
        for p in range(NC // 2):
            ls = slice(p * LANES, (p + 1) * LANES)
            cols.append(jnp.where(first_half, x[2 * p * C:(2 * p + 1) * C, ls],
                                  x[(2 * p + 1) * C:(2 * p + 2) * C, ls]))
        return jnp.concatenate(cols, axis=1)

    same_b = jnp.where(same, 1.0, 0.0).astype(BF16)

    def blockdiag(xw):
        return jnp.concatenate([xw.astype(BF16)] * NC, axis=0) * same_b

    def head_prep(h, done):
        hs = slice(h * DH, (h + 1) * DH)
        qb = q_ref[:, hs]
        q = qb.astype(F32)
        k = k_ref[:, hs].astype(F32)
        v = v_ref[:, hs].astype(F32)
        beta = jnp.broadcast_to(beta_all[:, h:h + 1], (T, DH))
        gcol = jnp.broadcast_to(gc_col_all[:, H + h:H + h + 1], (T, LANES))
        egcol = jnp.exp(gcol)
        grow = gc_rows[H + h:H + h + 1, :]
        gcol_w = widen(jnp.concatenate([gcol] * (NC // 2), axis=1))
        dec_w = jnp.exp(jnp.where(causal_w, gcol_w - grow, -1e30))

        kb = k * beta
        kT = k.T
        yield
        mm = _dot(jnp.concatenate([kb.astype(BF16), qb], axis=0), kT.astype(BF16))
        yield
        a_w = jnp.where(strict_w, widen(mm[:T]) * dec_w, 0.0)
        attn_w = widen(mm[T:]) * dec_w

        r_w = -a_w
        p_w = _dot(a_w.astype(BF16), blockdiag(a_w))
        yield
        n_sq = 5
        for s in range(n_sq):
            pbd = blockdiag(p_w)
            if s < n_sq - 1:
                rp = _dot(jnp.concatenate([r_w, p_w], axis=0).astype(BF16), pbd)
                r_w = r_w + p_w + rp[:C]
                p_w = rp[C:]
            else:
                r_w = r_w + p_w + _dot(r_w.astype(BF16), pbd)
            yield

        rhs = jnp.concatenate([v * beta, kb * egcol], axis=1)
        sol = rhs + _dot(blockdiag(r_w), rhs.astype(BF16))
        yield
        u0 = sol[:, :DH]
        qd = q * egcol
        wq = [jnp.concatenate([sol[c * C:(c + 1) * C, DH:], qd[c * C:(c + 1) * C]], axis=0).astype(BF16)
              for c in range(NC)]
        kd_t = kT * dec_w[C - 1:C, :]
        ak = [jnp.concatenate([attn_w[:, p * LANES:(p + 1) * LANES], kd_t[:, p * LANES:(p + 1) * LANES]],
                              axis=0).astype(BF16) for p in range(NC // 2)]
        eg = [eg_rows[H + h:H + h + 1, c * C + C - 1:c * C + C] for c in range(NC)]
        done[h] = (u0, wq, ak, eg)

    def pair_diag(x1, x2):
        z = jnp.zeros_like(x1)
        return jnp.concatenate([jnp.concatenate([x1, z], axis=1), jnp.concatenate([z, x2], axis=1)], axis=0)

    def pair_recurrence(hp, prep):
        hs = (2 * hp, 2 * hp + 1)
        pr = [prep[h] for h in hs]
        st = [state[h] for h in hs]
        outs = [[], []]
        for c in range(NC):
            res = _dot(jnp.concatenate([pr[0][1][c], pr[1][1][c]], axis=1),
                       pair_diag(st[0].astype(BF16), st[1].astype(BF16)))
            yield
            u_pairs = []
            for i in range(2):
                u = pr[i][0][c * C:(c + 1) * C] - res[:C, i * DH:(i + 1) * DH]
                zu = jnp.zeros_like(u)
                u_pairs.append(jnp.concatenate([u, zu] if c % 2 == 0 else [zu, u], axis=0).astype(BF16))
            res2 = _dot(jnp.concatenate([pr[0][2][c // 2], pr[1][2][c // 2]], axis=1),
                        pair_diag(u_pairs[0], u_pairs[1]))
            yield
            for i in range(2):
                ls = slice(i * DH, (i + 1) * DH)
                outs[i].append(res[C:, ls] + res2[:C, ls])
                st[i] = st[i] * pr[i][3][c] + res2[C:, ls]
        for i, h in enumerate(hs):
            state[h] = st[i]
            o = jnp.concatenate(outs[i], axis=0)
            o = o * lax.rsqrt(jnp.mean(o * o, axis=-1, keepdims=True) + EPS) * ng_ref[...]
            zz = z_ref[:, h * DH:(h + 1) * DH].astype(F32)
            out_ref[:, h * DH:(h + 1) * DH] = (o * _silu(zz)).astype(BF16)

    def lockstep(gens):
        gens = list(gens)
        while gens:
            gens = [g for g in gens if next(g, True) is None]

    prep = [None] * H
    lockstep(head_prep(h, prep) for h in range(H))
    lockstep(pair_recurrence(hp, prep) for hp in range(H // 2))


def _deltanet(proj, ba, a_log, dt_bias, norm_g, bsz, seq):
    nt = seq // DN_T
    w = DN_WIDTH
    pad = lambda p: jnp.zeros((1, LANES), F32).at[0, DN_HEADS:2 * DN_HEADS].set(p)
    tok = lambda b, t: b * nt + t
    return pl.pallas_call(
        _deltanet_kernel,
        grid=(bsz, nt),
        in_specs=[pl.BlockSpec((DN_T, w), lambda b, t: (tok(b, t), 0)),
                  pl.BlockSpec((DN_T, w), lambda b, t: (tok(b, t), 1)),
                  pl.BlockSpec((DN_T, w), lambda b, t: (tok(b, t), 2)),
                  pl.BlockSpec((DN_T, w), lambda b, t: (tok(b, t), COL_Z // w)),
                  pl.BlockSpec((DN_T, LANES), lambda b, t: (tok(b, t), 0)),
                  pl.BlockSpec((1, LANES), lambda b, t: (0, 0)),
                  pl.BlockSpec((1, LANES), lambda b, t: (0, 0)),
                  pl.BlockSpec((1, DN_HEAD_DIM), lambda b, t: (0, 0))],
        out_specs=pl.BlockSpec((DN_T, w), lambda b, t: (tok(b, t), 0)),
        out_shape=jax.ShapeDtypeStruct((bsz * seq, w), BF16),
        scratch_shapes=[pltpu.VMEM((DN_HEADS, DN_HEAD_DIM, DN_HEAD_DIM), F32)],
        compiler_params=_params(("arbitrary", "arbitrary")),
        name="deltanet",
    )(proj, proj, proj, proj, ba, pad(a_log), pad(dt_bias), norm_g.reshape(1, DN_HEAD_DIM))


def _conformer_kernel(val_ref, gl_ref, cw_ref, lg_ref, lb_ref, out_ref, ubuf, cbuf):
    t = pl.program_id(1)
    T = CF_T
    NB = CF_WIDTH // LANES

    @pl.when(t == 0)
    def _():
        ubuf[:, 0:CF_HALO, :] = jnp.zeros((NB, CF_HALO, LANES), F32)

    @pl.when(t > 0)
    def _():
        ubuf[:, 0:CF_HALO, :] = ubuf[:, T:T + CF_HALO, :]

    off = CF_HALO - (CF_KERNEL - 1)
    n = T // SUBLANES
    for cb in range(NB):
        cs = slice(cb * LANES, (cb + 1) * LANES)
        ucol = ubuf.at[cb]
        ucol[CF_HALO:CF_HALO + T, :] = val_ref[:, cs].astype(F32) * _sigmoid(gl_ref[:, cs].astype(F32))
        acc = [None] * SUBLANES
        for s in range(off, off + SUBLANES - 1 + CF_KERNEL):
            xs = ucol[pl.ds(s, n, stride=SUBLANES), :]
            for p in range(SUBLANES):
                j = s - off - p
                if 0 <= j < CF_KERNEL:
                    term = xs * cw_ref[j:j + 1, cs]
                    acc[p] = term if acc[p] is None else acc[p] + term
        for p in range(SUBLANES):
            cbuf.at[cb][pl.ds(p, n, stride=SUBLANES), :] = acc[p]

    y = jnp.concatenate([cbuf[cb] for cb in range(NB)], axis=1)
    mu = jnp.mean(y, axis=-1, keepdims=True)
    yc = y - mu
    var = jnp.mean(yc * yc, axis=-1, keepdims=True)
    yn = yc * lax.rsqrt(var + EPS) * lg_ref[...] + lb_ref[...]
    out_ref[...] = _silu(yn).astype(BF16)


def _conformer(proj, conv_w, ln_g, ln_b, bsz, seq):
    nt = seq // CF_T
    tok = lambda b, t: b * nt + t
    return pl.pallas_call(
        _conformer_kernel,
        grid=(bsz, nt),
        in_specs=[pl.BlockSpec((CF_T, CF_WIDTH), lambda b, t: (tok(b, t), COL_GLU // CF_WIDTH)),
                  pl.BlockSpec((CF_T, CF_WIDTH), lambda b, t: (tok(b, t), COL_GLU // CF_WIDTH + 1)),
                  pl.BlockSpec((CF_KERNEL, CF_WIDTH), lambda b, t: (0, 0)),
                  pl.BlockSpec((1, CF_WIDTH), lambda b, t: (0, 0)),
                  pl.BlockSpec((1, CF_WIDTH), lambda b, t: (0, 0))],
        out_specs=pl.BlockSpec((CF_T, CF_WIDTH), lambda b, t: (tok(b, t), 0)),
        out_shape=jax.ShapeDtypeStruct((bsz * seq, CF_WIDTH), BF16),
        scratch_shapes=[pltpu.VMEM((CF_WIDTH // LANES, CF_T + CF_HALO, LANES), F32),
                        pltpu.VMEM((CF_WIDTH // LANES, CF_T, LANES), F32)],
        compiler_params=_params(("arbitrary", "arbitrary")),
        name="conformer",
    )(proj, proj, conv_w, ln_g.reshape(1, CF_WIDTH), ln_b.reshape(1, CF_WIDTH))


def _merge_kernel(o_ref, u_ref, ga_ref, gb_ref, x_ref, mod_ref, g2_ref, wdo_ref, wco_ref, wout_ref,
                  x1_ref, hn2_ref):
    merged = []
    for r in range(0, MERGE_TM, ROW_SUB):
        rs = slice(r, r + ROW_SUB)
        branch_a = _dot(o_ref[rs, :], wdo_ref[...])
        branch_b = _dot(u_ref[rs, :], wco_ref[...])
        merged.append((_sigmoid(ga_ref[rs, :].astype(F32)) * branch_a
                       + _sigmoid(gb_ref[rs, :].astype(F32)) * branch_b).astype(BF16))
    for idx, r in enumerate(range(0, MERGE_TM, ROW_SUB)):
        rs = slice(r, r + ROW_SUB)
        mix = _dot(merged[idx], wout_ref[...])
        x1 = x_ref[rs, :] + mod_ref[2:3, :] * mix
        x1_ref[rs, :] = x1
        hn2 = _modulated_rmsnorm(x1, g2_ref[...], mod_ref[3:4, :], mod_ref[4:5, :])
        hn2_ref[rs, :] = hn2.astype(BF16)


def _merge(o_gated, u_act, proj, x2d, mod6, g2, w_dn_o, w_cf_o, w_out, seq):
    m, d = x2d.shape
    tiles_per_seq = seq // MERGE_TM
    const = lambda shape: pl.BlockSpec(shape, lambda i: (0, 0), pipeline_mode=pl.Buffered(1))
    return pl.pallas_call(
        _merge_kernel,
        grid=(m // MERGE_TM,),
        in_specs=[pl.BlockSpec((MERGE_TM, DN_WIDTH), lambda i: (i, 0)),
                  pl.BlockSpec((MERGE_TM, CF_WIDTH), lambda i: (i, 0)),
                  pl.BlockSpec((MERGE_TM, d), lambda i: (i, COL_GATE_A // d)),
                  pl.BlockSpec((MERGE_TM, d), lambda i: (i, COL_GATE_B // d)),
                  pl.BlockSpec((MERGE_TM, d), lambda i: (i, 0)),
                  pl.BlockSpec((None, 6, d), lambda i: (i // tiles_per_seq, 0, 0)),
                  pl.BlockSpec((1, d), lambda i: (0, 0)),
                  const((DN_WIDTH, d)),
                  const((CF_WIDTH, d)),
                  const((d, d))],
        out_specs=[pl.BlockSpec((MERGE_TM, d), lambda i: (i, 0)),
                   pl.BlockSpec((MERGE_TM, d), lambda i: (i, 0))],
        out_shape=[jax.ShapeDtypeStruct((m, d), F32),
                   jax.ShapeDtypeStruct((m, d), BF16)],
        compiler_params=_params(("arbitrary",)),
        name="merge",
    )(o_gated, u_act, proj, proj, x2d, mod6, g2, w_dn_o, w_cf_o, w_out)


def _ffn_kernel(hn_ref, x1_ref, mod_ref, gf_ref, wg_ref, wu_ref, cw_ref, wd_ref, out_ref,
                gbuf, carry, *, tiles_per_seq):
    i = pl.program_id(0)
    j = pl.program_id(1)
    nf = pl.num_programs(1)
    TM = FFN_TM

    @pl.when(j == 0)
    def _():
        out_ref[...] = jnp.zeros_like(out_ref)

    @pl.when(i == 0)
    def _():
        carry[j] = jnp.zeros((SUBLANES, FFN_TF), F32)

    keep = (i % tiles_per_seq != 0).astype(F32)
    gbuf[0:SUBLANES, :] = carry[j] * keep

    hs = []
    for r in range(0, TM, ROW_SUB):
        hn = hn_ref[r:r + ROW_SUB, :]
        gbuf[SUBLANES + r:SUBLANES + r + ROW_SUB, :] = _dot(hn, wg_ref[...])
        up = _dot(hn, wu_ref[...])
        acc = None
        for tap in range(FFN_CONV):
            o = SUBLANES + r - (FFN_CONV - 1) + tap
            term = gbuf[o:o + ROW_SUB, :] * cw_ref[tap:tap + 1, :]
            acc = term if acc is None else acc + term
        hs.append((_silu(acc) * up).astype(BF16))
    for idx, r in enumerate(range(0, TM, ROW_SUB)):
        out_ref[r:r + ROW_SUB, :] += _dot(hs[idx], wd_ref[...])

    carry[j] = gbuf[TM:TM + SUBLANES, :]

    @pl.when(j == nf - 1)
    def _():
        x2 = x1_ref[...] + mod_ref[5:6, :] * out_ref[...]
        ms = jnp.mean(x2 * x2, axis=-1, keepdims=True)
        out_ref[...] = x2 * lax.rsqrt(ms + EPS) * gf_ref[...]


def _ffn(hn2, x1, mod6, gf, w_up, conv_w, w_down, seq):
    m, d = x1.shape
    nf = FFN_DIM // FFN_TF
    tiles_per_seq = seq // FFN_TM
    return pl.pallas_call(
        functools.partial(_ffn_kernel, tiles_per_seq=tiles_per_seq),
        grid=(m // FFN_TM, nf),
        in_specs=[pl.BlockSpec((FFN_TM, d), lambda i, j: (i, 0)),
                  pl.BlockSpec((FFN_TM, d), lambda i, j: (i, 0)),
                  pl.BlockSpec((None, 6, d), lambda i, j: (i // tiles_per_seq, 0, 0)),
                  pl.BlockSpec((1, d), lambda i, j: (0, 0)),
                  pl.BlockSpec((d, FFN_TF), lambda i, j: (0, j)),
                  pl.BlockSpec((d, FFN_TF), lambda i, j: (0, nf + j)),
                  pl.BlockSpec((FFN_CONV, FFN_TF), lambda i, j: (0, j)),
                  pl.BlockSpec((FFN_TF, d), lambda i, j: (j, 0))],
        out_specs=pl.BlockSpec((FFN_TM, d), lambda i, j: (i, 0)),
        out_shape=jax.ShapeDtypeStruct((m, d), F32),
        scratch_shapes=[pltpu.VMEM((FFN_TM + SUBLANES, FFN_TF), F32),
                        pltpu.VMEM((nf, SUBLANES, FFN_TF), F32)],
        compiler_params=_params(("arbitrary", "arbitrary")),
        name="ffn",
    )(hn2, x1, mod6, gf, w_up, w_up, conv_w, w_down)


def kernel(x, c, w_ada, b_ada, norm1_g, w_in, dn_conv_w, dn_a_log, dn_dt_bias, dn_norm_g, dn_w_o,
           cf_conv_w, cf_ln_g, cf_ln_b, cf_w_o, w_out, norm2_g, ffn_w_up, ffn_conv_w, ffn_w_down,
           final_norm_g):
    bsz, seq, d = x.shape
    assert w_ada.shape[0] == 1, "single-layer block"
    x2d = x.reshape(bsz * seq, d)
    for l in range(1):
        mod6 = _ada(c, w_ada[l], b_ada[l]).reshape(bsz, 6, d)
        w_in_t = jnp.transpose(w_in[l])
        w_main = _repack_w_in(w_in_t)
        wba = jnp.pad(jnp.transpose(w_in_t[COL_GLU:COL_GLU + 2 * DN_HEADS]),
                      ((0, 0), (0, LANES - 2 * DN_HEADS))).astype(BF16)
        proj, ba = _inproj(x2d, mod6, norm1_g[l].reshape(1, d), w_main, wba, dn_conv_w[l], seq)
        o_gated = _deltanet(proj, ba, dn_a_log[l], dn_dt_bias[l], dn_norm_g[l], bsz, seq)
        u_act = _conformer(proj, cf_conv_w[l], cf_ln_g[l], cf_ln_b[l], bsz, seq)
        x1, hn2 = _merge(o_gated, u_act, proj, x2d, mod6, norm2_g[l].reshape(1, d),
                         dn_w_o[l].astype(BF16), cf_w_o[l].astype(BF16), w_out[l].astype(BF16), seq)
        x2d = _ffn(hn2, x1, mod6, final_norm_g.reshape(1, d), ffn_w_up[l].astype(BF16), ffn_conv_w[l],
                   ffn_w_down[l].astype(BF16), seq)
    return x2d.reshape(bsz, seq, d)
```

```python
import functools

import jax
import jax.numpy as jnp
from jax import lax
from jax.experimental import pallas as pl
from jax.experimental.pallas import tpu as pltpu

F32 = jnp.float32
BF16 = jnp.bfloat16

D_MODEL = 2048
DN_HEADS = 8
DN_HEAD_DIM = 128
DN_WIDTH = DN_HEADS * DN_HEAD_DIM
DN_CONV = 4
DN_CHUNK = 64
CF_WIDTH = 1024
CF_KERNEL = 31
FFN_DIM = 5632
FFN_CONV = 3
EPS = 1e-6

LANES = 128
SUBLANES = 8
VMEM_LIMIT = 60000 * 1024

COL_Z = 3 * DN_WIDTH
COL_GLU = COL_Z + DN_WIDTH
COL_GATE_A = COL_GLU + 2 * CF_WIDTH
COL_GATE_B = COL_GATE_A + D_MODEL
N_MAIN = COL_GATE_B + D_MODEL

ROW_SUB = 256
REPACK_COLS = 512
ADA_TN = 1024
INPROJ_TM = 1024
INPROJ_TN = 1024
DN_T = 256
CF_T = 256
CF_HALO = 32
MERGE_TM = 512
FFN_TM = 1024
FFN_TF = 512


def _dot(a, b):
    return jnp.dot(a, b, preferred_element_type=F32)


def _sigmoid(x):
    return 0.5 * jnp.tanh(0.5 * x) + 0.5


def _silu(x):
    h = 0.5 * x
    return h * jnp.tanh(h) + h


def _params(sem):
    return pltpu.CompilerParams(dimension_semantics=sem, vmem_limit_bytes=VMEM_LIMIT)


def _ada_kernel(c_ref, w_ref, b_ref, o_ref):
    c = c_ref[...]
    cact = _silu(c).astype(BF16)
    o_ref[...] = _dot(cact, w_ref[...].astype(BF16)) + b_ref[...]


def _ada(c, w, b):
    bsz, d = c.shape
    n = w.shape[1]
    return pl.pallas_call(
        _ada_kernel,
        grid=(n // ADA_TN,),
        in_specs=[pl.BlockSpec((bsz, d), lambda j: (0, 0)),
                  pl.BlockSpec((d, ADA_TN), lambda j: (0, j)),
                  pl.BlockSpec((1, ADA_TN), lambda j: (0, j))],
        out_specs=pl.BlockSpec((bsz, ADA_TN), lambda j: (0, j)),
        out_shape=jax.ShapeDtypeStruct((bsz, n), F32),
        compiler_params=_params(("arbitrary",)),
        name="ada",
    )(c, w, b.reshape(1, n))


def _repack_kernel(wt_ref, wt_ba_ref, main_ref, ba_ref):
    main_ref[...] = wt_ref[...].T.astype(BF16)

    @pl.when(pl.program_id(0) == 0)
    def _():
        lane = lax.broadcasted_iota(jnp.int32, ba_ref.shape, 1)
        ba_ref[...] = jnp.where(lane < 2 * DN_HEADS, wt_ba_ref[...].T, 0.0).astype(BF16)


def _repack_w_in(w_t):
    n, d = w_t.shape
    lo, skip = COL_GLU, 2 * DN_HEADS
    assert lo % REPACK_COLS == 0 and skip % SUBLANES == 0
    return pl.pallas_call(
        _repack_kernel,
        grid=(N_MAIN // REPACK_COLS,),
        in_specs=[pl.BlockSpec((pl.Element(REPACK_COLS), pl.Element(d)),
                               lambda c: ((c * (REPACK_COLS // SUBLANES)
                                           + jnp.where(c >= lo // REPACK_COLS, skip // SUBLANES, 0)) * SUBLANES, 0)),
                  pl.BlockSpec((LANES, d), lambda c: (lo // LANES, 0))],
        out_specs=[pl.BlockSpec((d, REPACK_COLS), lambda c: (0, c)),
                   pl.BlockSpec((d, LANES), lambda c: (0, 0))],
        out_shape=[jax.ShapeDtypeStruct((d, N_MAIN), BF16),
                   jax.ShapeDtypeStruct((d, LANES), BF16)],
        compiler_params=_params(("arbitrary",)),
        name="repack",
    )(w_t, w_t)


def _modulated_rmsnorm(x, g, shift, scale):
    ms = jnp.mean(x * x, axis=-1, keepdims=True)
    y = x * lax.rsqrt(ms + EPS) * g
    return y * (1.0 + scale) + shift


def _inproj_kernel(x_ref, mod_ref, g_ref, w_ref, wba_ref, cw_ref, out_ref, ba_ref, hn_ref, cbuf, obuf, carry,
                   *, tiles_per_seq):
    i = pl.program_id(0)
    j = pl.program_id(1)
    TM = INPROJ_TM

    @pl.when(jnp.logical_and(i == 0, j == 0))
    def _():
        carry[...] = jnp.zeros_like(carry)

    def conv_tile(slot, l2_scale, first=False):
        keep = (i % tiles_per_seq != 0).astype(F32)
        cbuf[:, 0:SUBLANES, :] = carry[slot] * keep
        n = ROW_SUB // SUBLANES
        for r in range(0, TM, ROW_SUB):
            if first:
                hn = _modulated_rmsnorm(x_ref[r:r + ROW_SUB, :], g_ref[...], mod_ref[0:1, :], mod_ref[1:2, :])
                hn_ref[r:r + ROW_SUB, :] = hn.astype(BF16)
            res = _dot(hn_ref[r:r + ROW_SUB, :], w_ref[...])
            for cb in range(INPROJ_TN // LANES):
                cs = slice(cb * LANES, (cb + 1) * LANES)
                col = cbuf.at[cb]
                col[SUBLANES + r:SUBLANES + r + ROW_SUB, :] = res[:, cs]
                base = SUBLANES + r - (DN_CONV - 1)
                acc = [None] * SUBLANES
                for s in range(SUBLANES - 1 + DN_CONV):
                    xs = col[pl.ds(base + s, n, stride=SUBLANES), :]
                    for p in range(SUBLANES):
                        tap = s - p
                        if 0 <= tap < DN_CONV:
                            term = xs * cw_ref[tap:tap + 1, cs]
                            acc[p] = term if acc[p] is None else acc[p] + term
                for p in range(SUBLANES):
                    y = _silu(acc[p])
                    if l2_scale is not None:
                        y = y * (lax.rsqrt(jnp.sum(y * y, axis=-1, keepdims=True) + EPS) * l2_scale)
                    obuf.at[cb][pl.ds(r + p, n, stride=SUBLANES), :] = y
                out_ref[r:r + ROW_SUB, cs] = obuf[cb, r:r + ROW_SUB, :].astype(BF16)
        carry[slot] = cbuf[:, TM:TM + SUBLANES, :]
        if first:
            ba_ref[...] = _dot(hn_ref[...], wba_ref[...])

    pl.when(j == 0)(functools.partial(conv_tile, 0, DN_HEAD_DIM ** -0.5, first=True))
    pl.when(j == 1)(functools.partial(conv_tile, 1, 1.0))
    pl.when(j == 2)(functools.partial(conv_tile, 2, None))

    @pl.when(j >= 3)
    def _():
        for r in range(0, TM, ROW_SUB):
            out_ref[r:r + ROW_SUB, :] = _dot(hn_ref[r:r + ROW_SUB, :], w_ref[...]).astype(BF16)


def _inproj(x2d, mod6, g1, w_main, wba, conv_w, seq):
    m, d = x2d.shape
    n = w_main.shape[1]
    assert INPROJ_TN == DN_WIDTH and DN_HEAD_DIM == LANES, "q / k / v: one column tile each, one head per lane tile"
    tiles_per_seq = seq // INPROJ_TM
    n_conv = 3
    nb = INPROJ_TN // LANES
    return pl.pallas_call(
        functools.partial(_inproj_kernel, tiles_per_seq=tiles_per_seq),
        grid=(m // INPROJ_TM, n // INPROJ_TN),
        in_specs=[pl.BlockSpec((INPROJ_TM, d), lambda i, j: (i, 0)),
                  pl.BlockSpec((None, 6, d), lambda i, j: (i // tiles_per_seq, 0, 0)),
                  pl.BlockSpec((1, d), lambda i, j: (0, 0)),
                  pl.BlockSpec((d, INPROJ_TN), lambda i, j: (0, j)),
                  pl.BlockSpec((d, LANES), lambda i, j: (0, 0)),
                  pl.BlockSpec((DN_CONV, INPROJ_TN), lambda i, j: (0, jnp.minimum(j, n_conv - 1)))],
        out_specs=[pl.BlockSpec((INPROJ_TM, INPROJ_TN), lambda i, j: (i, j)),
                   pl.BlockSpec((INPROJ_TM, LANES), lambda i, j: (i, 0))],
        out_shape=[jax.ShapeDtypeStruct((m, n), BF16),
                   jax.ShapeDtypeStruct((m, LANES), F32)],
        scratch_shapes=[pltpu.VMEM((INPROJ_TM, d), BF16),
                        pltpu.VMEM((nb, INPROJ_TM + SUBLANES, LANES), F32),
                        pltpu.VMEM((nb, INPROJ_TM, LANES), F32),
                        pltpu.VMEM((n_conv, nb, SUBLANES, LANES), F32)],
        compiler_params=_params(("arbitrary", "arbitrary")),
        name="inproj",
    )(x2d, mod6, g1, w_main, wba, conv_w)


def _split3(x):
    h1 = x.astype(BF16)
    r1 = x - h1.astype(F32)
    h2 = r1.astype(BF16)
    h3 = (r1 - h2.astype(F32)).astype(BF16)
    return h1, h2, h3


def _deltanet_kernel(q_ref, k_ref, v_ref, z_ref, ba_ref, alog_ref, dtb_ref, ng_ref, out_ref, state):
    t = pl.program_id(1)
    T, C, H, DH = DN_T, DN_CHUNK, DN_HEADS, DN_HEAD_DIM
    NC = T // C

    @pl.when(t == 0)
    def _():
        state[...] = jnp.zeros_like(state)

    ba = ba_ref[...]
    beta_all = _sigmoid(ba)
    sp_in = ba + dtb_ref[...]
    softplus = jnp.maximum(sp_in, 0.0) + jnp.log(1.0 + jnp.exp(-jnp.abs(sp_in)))
    g_all = -jnp.exp(alog_ref[...]) * softplus

    row = lax.broadcasted_iota(jnp.int32, (T, T), 0)
    col = lax.broadcasted_iota(jnp.int32, (T, T), 1)
    same = (row // C) == (col // C)
    lower = jnp.where(jnp.logical_and(same, col <= row), 1.0, 0.0).astype(BF16)
    upper = jnp.where(jnp.logical_and(same, row <= col), 1.0, 0.0).astype(BF16)

    g1, g2, g3 = _split3(g_all)
    gc_col_all = _dot(lower, g1) + _dot(lower, g2) + _dot(lower, g3)
    g_rows = g_all.T[0:2 * H, :]
    r1, r2, r3 = _split3(g_rows)
    gc_rows = _dot(r1, upper) + _dot(r2, upper) + _dot(r3, upper)
    eg_rows = jnp.exp(gc_rows)

    wrow = lax.broadcasted_iota(jnp.int32, (C, T), 0)
    wcol = lax.broadcasted_iota(jnp.int32, (C, T), 1) % C
    causal_w = wcol <= wrow
    strict_w = wcol < wrow
    first_half = lax.broadcasted_iota(jnp.int32, (C, LANES), 1) < C

    def widen(x):
        cols = []
        for p in range(NC // 2):
            ls = slice(p * LANES, (p + 1) * LANES)
            cols.append(jnp.where(first_half, x[2 * p * C:(2 * p + 1) * C, ls],
                                  x[(2 * p + 1) * C:(2 * p + 2) * C, ls]))
        return jnp.concatenate(cols, axis=1)

    same_b = jnp.where(same, 1.0, 0.0).astype(BF16)

    def blockdiag(xw):
        return jnp.concatenate([xw.astype(BF16)] * NC, axis=0) * same_b

    def head_prep(h, done):
        hs = slice(h * DH, (h + 1) * DH)
        qb = q_ref[:, hs]
        q = qb.astype(F32)
        k = k_ref[:, hs].astype(F32)
        v = v_ref[:, hs].astype(F32)
        beta = jnp.broadcast_to(beta_all[:, h:h + 1], (T, DH))
        gcol = jnp.broadcast_to(gc_col_all[:, H + h:H + h + 1], (T, LANES))
        egcol = jnp.exp(gcol)
        grow = gc_rows[H + h:H + h + 1, :]
        gcol_w = widen(jnp.concatenate([gcol] * (NC // 2), axis=1))
        dec_w = jnp.exp(jnp.where(causal_w, gcol_w - grow, -1e30))

        kb = k * beta
        kT = k.T
        yield
        mm = _dot(jnp.concatenate([kb.astype(BF16), qb], axis=0), kT.astype(BF16))
        yield
        a_w = jnp.where(strict_w, widen(mm[:T]) * dec_w, 0.0)
        attn_w = widen(mm[T:]) * dec_w

        r_w = -a_w
        p_w = _dot(a_w.astype(BF16), blockdiag(a_w))
        yield
        n_sq = 5
        for s in range(n_sq):
            pbd = blockdiag(p_w)
            if s < n_sq - 1:
                rp = _dot(jnp.concatenate([r_w, p_w], axis=0).astype(BF16), pbd)
                r_w = r_w + p_w + rp[:C]
                p_w = rp[C:]
            else:
                r_w = r_w + p_w + _dot(r_w.astype(BF16), pbd)
            yield

        rhs = jnp.concatenate([v * beta, kb * egcol], axis=1)
        sol = rhs + _dot(blockdiag(r_w), rhs.astype(BF16))
        yield
        u0 = sol[:, :DH]
        qd = q * egcol
        wq = [jnp.concatenate([sol[c * C:(c + 1) * C, DH:], qd[c * C:(c + 1) * C]], axis=0).astype(BF16)
              for c in range(NC)]
        kd_t = kT * dec_w[C - 1:C, :]
        ak = [jnp.concatenate([attn_w[:, p * LANES:(p + 1) * LANES], kd_t[:, p * LANES:(p + 1) * LANES]],
                              axis=0).astype(BF16) for p in range(NC // 2)]
        eg = [eg_rows[H + h:H + h + 1, c * C + C - 1:c * C + C] for c in range(NC)]
        done[h] = (u0, wq, ak, eg)

    def pair_diag(x1, x2):
        z = jnp.zeros_like(x1)
        return jnp.concatenate([jnp.concatenate([x1, z], axis=1), jnp.concatenate([z, x2], axis=1)], axis=0)

    def pair_recurrence(hp, prep):
        hs = (2 * hp, 2 * hp + 1)
        pr = [prep[h] for h in hs]
        st = [state[h] for h in hs]
        outs = [[], []]
        for c in range(NC):
            res = _dot(jnp.concatenate([pr[0][1][c], pr[1][1][c]], axis=1),
                       pair_diag(st[0].astype(BF16), st[1].astype(BF16)))
            yield
            u_pairs = []
            for i in range(2):
                u = pr[i][0][c * C:(c + 1) * C] - res[:C, i * DH:(i + 1) * DH]
                zu = jnp.zeros_like(u)
                u_pairs.append(jnp.concatenate([u, zu] if c % 2 == 0 else [zu, u], axis=0).astype(BF16))
            res2 = _dot(jnp.concatenate([pr[0][2][c // 2], pr[1][2][c // 2]], axis=1),
                        pair_diag(u_pairs[0], u_pairs[1]))
            yield
            for i in range(2):
                ls = slice(i * DH, (i + 1) * DH)
                outs[i].append(res[C:, ls] + res2[:C, ls])
                st[i] = st[i] * pr[i][3][c] + res2[C:, ls]
        for i, h in enumerate(hs):
            state[h] = st[i]
            o = jnp.concatenate(outs[i], axis=0)
            o = o * lax.rsqrt(jnp.mean(o * o, axis=-1, keepdims=True) + EPS) * ng_ref[...]
            zz = z_ref[:, h * DH:(h + 1) * DH].astype(F32)
            out_ref[:, h * DH:(h + 1) * DH] = (o * _silu(zz)).astype(BF16)

    def lockstep(gens):
        gens = list(gens)
        while gens:
            gens = [g for g in gens if next(g, True) is None]

    prep = [None] * H
    lockstep(head_prep(h, prep) for h in range(H))
    lockstep(pair_recurrence(hp, prep) for hp in range(H // 2))


def _deltanet(proj, ba, a_log, dt_bias, norm_g, bsz, seq):
    nt = seq // DN_T
    w = DN_WIDTH
    pad = lambda p: jnp.zeros((1, LANES), F32).at[0, DN_HEADS:2 * DN_HEADS].set(p)
    tok = lambda b, t: b * nt + t
    return pl.pallas_call(
        _deltanet_kernel,
        grid=(bsz, nt),
        in_specs=[pl.BlockSpec((DN_T, w), lambda b, t: (tok(b, t), 0)),
                  pl.BlockSpec((DN_T, w), lambda b, t: (tok(b, t), 1)),
                  pl.BlockSpec((DN_T, w), lambda b, t: (tok(b, t), 2)),
                  pl.BlockSpec((DN_T, w), lambda b, t: (tok(b, t), COL_Z // w)),
                  pl.BlockSpec((DN_T, LANES), lambda b, t: (tok(b, t), 0)),
                  pl.BlockSpec((1, LANES), lambda b, t: (0, 0)),
                  pl.BlockSpec((1, LANES), lambda b, t: (0, 0)),
                  pl.BlockSpec((1, DN_HEAD_DIM), lambda b, t: (0, 0))],
        out_specs=pl.BlockSpec((DN_T, w), lambda b, t: (tok(b, t), 0)),
        out_shape=jax.ShapeDtypeStruct((bsz * seq, w), BF16),
        scratch_shapes=[pltpu.VMEM((DN_HEADS, DN_HEAD_DIM, DN_HEAD_DIM), F32)],
        compiler_params=_params(("arbitrary", "arbitrary")),
        name="deltanet",
    )(proj, proj, proj, proj, ba, pad(a_log), pad(dt_bias), norm_g.reshape(1, DN_HEAD_DIM))


def _conformer_kernel(val_ref, gl_ref, cw_ref, lg_ref, lb_ref, out_ref, ubuf, cbuf):
    t = pl.program_id(1)
    T = CF_T
    NB = CF_WIDTH // LANES

    @pl.when(t == 0)
    def _():
        ubuf[:, 0:CF_HALO, :] = jnp.zeros((NB, CF_HALO, LANES), F32)

    @pl.when(t > 0)
    def _():
        ubuf[:, 0:CF_HALO, :] = ubuf[:, T:T + CF_HALO, :]

    off = CF_HALO - (CF_KERNEL - 1)
    n = T // SUBLANES
    for cb in range(NB):
        cs = slice(cb * LANES, (cb + 1) * LANES)
        ucol = ubuf.at[cb]
        ucol[CF_HALO:CF_HALO + T, :] = val_ref[:, cs].astype(F32) * _sigmoid(gl_ref[:, cs].astype(F32))
        acc = [None] * SUBLANES
        for s in range(off, off + SUBLANES - 1 + CF_KERNEL):
            xs = ucol[pl.ds(s, n, stride=SUBLANES), :]
            for p in range(SUBLANES):
                j = s - off - p
                if 0 <= j < CF_KERNEL:
                    term = xs * cw_ref[j:j + 1, cs]
                    acc[p] = term if acc[p] is None else acc[p] + term
        for p in range(SUBLANES):
            cbuf.at[cb][pl.ds(p, n, stride=SUBLANES), :] = acc[p]

    y = jnp.concatenate([cbuf[cb] for cb in range(NB)], axis=1)
    mu = jnp.mean(y, axis=-1, keepdims=True)
    yc = y - mu
    var = jnp.mean(yc * yc, axis=-1, keepdims=True)
    yn = yc * lax.rsqrt(var + EPS) * lg_ref[...] + lb_ref[...]
    out_ref[...] = _silu(yn).astype(BF16)


def _conformer(proj, conv_w, ln_g, ln_b, bsz, seq):
    nt = seq // CF_T
    tok = lambda b, t: b * nt + t
    return pl.pallas_call(
        _conformer_kernel,
        grid=(bsz, nt),
        in_specs=[pl.BlockSpec((CF_T, CF_WIDTH), lambda b, t: (tok(b, t), COL_GLU // CF_WIDTH)),
                  pl.BlockSpec((CF_T, CF_WIDTH), lambda b, t: (tok(b, t), COL_GLU // CF_WIDTH + 1)),
                  pl.BlockSpec((CF_KERNEL, CF_WIDTH), lambda b, t: (0, 0)),
                  pl.BlockSpec((1, CF_WIDTH), lambda b, t: (0, 0)),
                  pl.BlockSpec((1, CF_WIDTH), lambda b, t: (0, 0))],
        out_specs=pl.BlockSpec((CF_T, CF_WIDTH), lambda b, t: (tok(b, t), 0)),
        out_shape=jax.ShapeDtypeStruct((bsz * seq, CF_WIDTH), BF16),
        scratch_shapes=[pltpu.VMEM((CF_WIDTH // LANES, CF_T + CF_HALO, LANES), F32),
                        pltpu.VMEM((CF_WIDTH // LANES, CF_T, LANES), F32)],
        compiler_params=_params(("arbitrary", "arbitrary")),
        name="conformer",
    )(proj, proj, conv_w, ln_g.reshape(1, CF_WIDTH), ln_b.reshape(1, CF_WIDTH))


def _merge_kernel(o_ref, u_ref, ga_ref, gb_ref, x_ref, mod_ref, g2_ref, wdo_ref, wco_ref, wout_ref,
                  x1_ref, hn2_ref):
    merged = []
    for r in range(0, MERGE_TM, ROW_SUB):
        rs = slice(r, r + ROW_SUB)
        branch_a = _dot(o_ref[rs, :], wdo_ref[...])
        branch_b = _dot(u_ref[rs, :], wco_ref[...])
        merged.append((_sigmoid(ga_ref[rs, :].astype(F32)) * branch_a
                       + _sigmoid(gb_ref[rs, :].astype(F32)) * branch_b).astype(BF16))
    for idx, r in enumerate(range(0, MERGE_TM, ROW_SUB)):
        rs = slice(r, r + ROW_SUB)
        mix = _dot(merged[idx], wout_ref[...])
        x1 = x_ref[rs, :] + mod_ref[2:3, :] * mix
        x1_ref[rs, :] = x1
        hn2 = _modulated_rmsnorm(x1, g2_ref[...], mod_ref[3:4, :], mod_ref[4:5, :])
        hn2_ref[rs, :] = hn2.astype(BF16)


def _merge(o_gated, u_act, proj, x2d, mod6, g2, w_dn_o, w_cf_o, w_out, seq):
    m, d = x2d.shape
    tiles_per_seq = seq // MERGE_TM
    const = lambda shape: pl.BlockSpec(shape, lambda i: (0, 0), pipeline_mode=pl.Buffered(1))
    return pl.pallas_call(
        _merge_kernel,
        grid=(m // MERGE_TM,),
        in_specs=[pl.BlockSpec((MERGE_TM, DN_WIDTH), lambda i: (i, 0)),
                  pl.BlockSpec((MERGE_TM, CF_WIDTH), lambda i: (i, 0)),
                  pl.BlockSpec((MERGE_TM, d), lambda i: (i, COL_GATE_A // d)),
                  pl.BlockSpec((MERGE_TM, d), lambda i: (i, COL_GATE_B // d)),
                  pl.BlockSpec((MERGE_TM, d), lambda i: (i, 0)),
                  pl.BlockSpec((None, 6, d), lambda i: (i // tiles_per_seq, 0, 0)),
                  pl.BlockSpec((1, d), lambda i: (0, 0)),
                  const((DN_WIDTH, d)),
                  const((CF_WIDTH, d)),
                  const((d, d))],
        out_specs=[pl.BlockSpec((MERGE_TM, d), lambda i: (i, 0)),
                   pl.BlockSpec((MERGE_TM, d), lambda i: (i, 0))],
        out_shape=[jax.ShapeDtypeStruct((m, d), F32),
                   jax.ShapeDtypeStruct((m, d), BF16)],
        compiler_params=_params(("arbitrary",)),
        name="merge",
    )(o_gated, u_act, proj, proj, x2d, mod6, g2, w_dn_o, w_cf_o, w_out)


def _ffn_kernel(hn_ref, x1_ref, mod_ref, gf_ref, wg_ref, wu_ref, cw_ref, wd_ref, out_ref,
                gbuf, carry, *, tiles_per_seq):
    i = pl.program_id(0)
    j = pl.program_id(1)
    nf = pl.num_programs(1)
    TM = FFN_TM

    @pl.when(j == 0)
    def _():
        out_ref[...] = jnp.zeros_like(out_ref)

    @pl.when(i == 0)
    def _():
        carry[j] = jnp.zeros((SUBLANES, FFN_TF), F32)

    keep = (i % tiles_per_seq != 0).astype(F32)
    gbuf[0:SUBLANES, :] = carry[j] * keep

    hs = []
    for r in range(0, TM, ROW_SUB):
        hn = hn_ref[r:r + ROW_SUB, :]
        gbuf[SUBLANES + r:SUBLANES + r + ROW_SUB, :] = _dot(hn, wg_ref[...])
        up = _dot(hn, wu_ref[...])
        acc = None
        for tap in range(FFN_CONV):
            o = SUBLANES + r - (FFN_CONV - 1) + tap
            term = gbuf[o:o + ROW_SUB, :] * cw_ref[tap:tap + 1, :]
            acc = term if acc is None else acc + term
        hs.append((_silu(acc) * up).astype(BF16))
    for idx, r in enumerate(range(0, TM, ROW_SUB)):
        out_ref[r:r + ROW_SUB, :] += _dot(hs[idx], wd_ref[...])

    carry[j] = gbuf[TM:TM + SUBLANES, :]

    @pl.when(j == nf - 1)
    def _():
        x2 = x1_ref[...] + mod_ref[5:6, :] * out_ref[...]
        ms = jnp.mean(x2 * x2, axis=-1, keepdims=True)
        out_ref[...] = x2 * lax.rsqrt(ms + EPS) * gf_ref[...]


def _ffn(hn2, x1, mod6, gf, w_up, conv_w, w_down, seq):
    m, d = x1.shape
    nf = FFN_DIM // FFN_TF
    tiles_per_seq = seq // FFN_TM
    return pl.pallas_call(
        functools.partial(_ffn_kernel, tiles_per_seq=tiles_per_seq),
        grid=(m // FFN_TM, nf),
        in_specs=[pl.BlockSpec((FFN_TM, d), lambda i, j: (i, 0)),
                  pl.BlockSpec((FFN_TM, d), lambda i, j: (i, 0), pipeline_mode=pl.Buffered(1)),
                  pl.BlockSpec((None, 6, d), lambda i, j: (i // tiles_per_seq, 0, 0)),
                  pl.BlockSpec((1, d), lambda i, j: (0, 0)),
                  pl.BlockSpec((d, FFN_TF), lambda i, j: (0, j)),
                  pl.BlockSpec((d, FFN_TF), lambda i, j: (0, nf + j)),
                  pl.BlockSpec((FFN_CONV, FFN_TF), lambda i, j: (0, j)),
                  pl.BlockSpec((FFN_TF, d), lambda i, j: (j, 0))],
        out_specs=pl.BlockSpec((FFN_TM, d), lambda i, j: (i, 0)),
        out_shape=jax.ShapeDtypeStruct((m, d), F32),
        scratch_shapes=[pltpu.VMEM((FFN_TM + SUBLANES, FFN_TF), F32),
                        pltpu.VMEM((nf, SUBLANES, FFN_TF), F32)],
        compiler_params=_params(("arbitrary", "arbitrary")),
        name="ffn",
    )(hn2, x1, mod6, gf, w_up, w_up, conv_w, w_down)


def kernel(x, c, w_ada, b_ada, norm1_g, w_in, dn_conv_w, dn_a_log, dn_dt_bias, dn_norm_g, dn_w_o,
           cf_conv_w, cf_ln_g, cf_ln_b, cf_w_o, w_out, norm2_g, ffn_w_up, ffn_conv_w, ffn_w_down,
           final_norm_g):
    bsz, seq, d = x.shape
    assert w_ada.shape[0] == 1, "single-layer block"
    x2d = x.reshape(bsz * seq, d)
    for l in range(1):
        mod6 = _ada(c, w_ada[l], b_ada[l]).reshape(bsz, 6, d)
        w_in_t = jnp.transpose(w_in[l])
        w_main, wba = _repack_w_in(w_in_t)
        proj, ba = _inproj(x2d, mod6, norm1_g[l].reshape(1, d), w_main, wba, dn_conv_w[l], seq)
        o_gated = _deltanet(proj, ba, dn_a_log[l], dn_dt_bias[l], dn_norm_g[l], bsz, seq)
        u_act = _conformer(proj, cf_conv_w[l], cf_ln_g[l], cf_ln_b[l], bsz, seq)
        x1, hn2 = _merge(o_gated, u_act, proj, x2d, mod6, norm2_g[l].reshape(1, d),
                         dn_w_o[l].astype(BF16), cf_w_o[l].astype(BF16), w_out[l].astype(BF16), seq)
        x2d = _ffn(hn2, x1, mod6, final_norm_g.reshape(1, d), ffn_w_up[l].astype(BF16), ffn_conv_w[l],
                   ffn_w_down[l].astype(BF16), seq)
    return x2d.reshape(bsz, seq, d)
```

```python
import functools

import jax
import jax.numpy as jnp
from jax import lax
from jax.experimental import pallas as pl
from jax.experimental.pallas import tpu as pltpu

F32 = jnp.float32
BF16 = jnp.bfloat16

D_MODEL = 2048
DN_HEADS = 8
DN_HEAD_DIM = 128
DN_WIDTH = DN_HEADS * DN_HEAD_DIM
DN_CONV = 4
DN_CHUNK = 64
CF_WIDTH = 1024
CF_KERNEL = 31
FFN_DIM = 5632
FFN_CONV = 3
EPS = 1e-6

LANES = 128
SUBLANES = 8
VMEM_LIMIT = 60000 * 1024

COL_Z = 3 * DN_WIDTH
COL_GLU = COL_Z + DN_WIDTH
COL_GATE_A = COL_GLU + 2 * CF_WIDTH
COL_GATE_B = COL_GATE_A + D_MODEL
N_MAIN = COL_GATE_B + D_MODEL

ROW_SUB = 256
REPACK_COLS = 512
ADA_TN = 1024
INPROJ_TM = 1024
INPROJ_TN = 1024
DN_T = 256
DN_ROWS = 2
CF_T = 256
CF_HALO = 32
MERGE_TM = 512
FFN_TM = 512
FFN_TF = 512


def _dot(a, b):
    return jnp.dot(a, b, preferred_element_type=F32)


def _sigmoid(x):
    return 0.5 * jnp.tanh(0.5 * x) + 0.5


def _silu(x):
    h = 0.5 * x
    return h * jnp.tanh(h) + h


def _params(sem):
    return pltpu.CompilerParams(dimension_semantics=sem, vmem_limit_bytes=VMEM_LIMIT)


def _ada_kernel(c_ref, w_ref, b_ref, o_ref):
    c = c_ref[...]
    cact = _silu(c).astype(BF16)
    o_ref[...] = _dot(cact, w_ref[...].astype(BF16)) + b_ref[...]


def _ada(c, w, b):
    bsz, d = c.shape
    n = w.shape[1]
    return pl.pallas_call(
        _ada_kernel,
        grid=(n // ADA_TN,),
        in_specs=[pl.BlockSpec((bsz, d), lambda j: (0, 0)),
                  pl.BlockSpec((d, ADA_TN), lambda j: (0, j)),
                  pl.BlockSpec((1, ADA_TN), lambda j: (0, j))],
        out_specs=pl.BlockSpec((bsz, ADA_TN), lambda j: (0, j)),
        out_shape=jax.ShapeDtypeStruct((bsz, n), F32),
        compiler_params=_params(("arbitrary",)),
        name="ada",
    )(c, w, b.reshape(1, n))


def _repack_kernel(wt_ref, wt_ba_ref, main_ref, ba_ref):
    main_ref[...] = wt_ref[...].T.astype(BF16)

    @pl.when(pl.program_id(0) == 0)
    def _():
        lane = lax.broadcasted_iota(jnp.int32, ba_ref.shape, 1)
        ba_ref[...] = jnp.where(lane < 2 * DN_HEADS, wt_ba_ref[...].T, 0.0).astype(BF16)


def _repack_w_in(w_t):
    n, d = w_t.shape
    lo, skip = COL_GLU, 2 * DN_HEADS
    assert lo % REPACK_COLS == 0 and skip % SUBLANES == 0
    return pl.pallas_call(
        _repack_kernel,
        grid=(N_MAIN // REPACK_COLS,),
        in_specs=[pl.BlockSpec((pl.Element(REPACK_COLS), pl.Element(d)),
                               lambda c: ((c * (REPACK_COLS // SUBLANES)
                                           + jnp.where(c >= lo // REPACK_COLS, skip // SUBLANES, 0)) * SUBLANES, 0)),
                  pl.BlockSpec((LANES, d), lambda c: (lo // LANES, 0))],
        out_specs=[pl.BlockSpec((d, REPACK_COLS), lambda c: (0, c)),
                   pl.BlockSpec((d, LANES), lambda c: (0, 0))],
        out_shape=[jax.ShapeDtypeStruct((d, N_MAIN), BF16),
                   jax.ShapeDtypeStruct((d, LANES), BF16)],
        compiler_params=_params(("arbitrary",)),
        name="repack",
    )(w_t, w_t)


def _modulated_rmsnorm(x, g, shift, scale):
    ms = jnp.mean(x * x, axis=-1, keepdims=True)
    y = x * lax.rsqrt(ms + EPS) * g
    return y * (1.0 + scale) + shift


def _inproj_kernel(x_ref, mod_ref, g_ref, w_ref, wba_ref, cw_ref, out_ref, ba_ref, hn_ref, cbuf, obuf, carry,
                   *, tiles_per_seq):
    i = pl.program_id(0)
    j = pl.program_id(1)
    TM = INPROJ_TM

    @pl.when(jnp.logical_and(i == 0, j == 0))
    def _():
        carry[...] = jnp.zeros_like(carry)

    def conv_tile(slot, l2_scale, first=False):
        keep = (i % tiles_per_seq != 0).astype(F32)
        cbuf[:, 0:SUBLANES, :] = carry[slot] * keep
        n = ROW_SUB // SUBLANES
        for r in range(0, TM, ROW_SUB):
            if first:
                hn = _modulated_rmsnorm(x_ref[r:r + ROW_SUB, :], g_ref[...], mod_ref[0:1, :], mod_ref[1:2, :])
                hn_ref[r:r + ROW_SUB, :] = hn.astype(BF16)
            res = _dot(hn_ref[r:r + ROW_SUB, :], w_ref[...])
            for cb in range(INPROJ_TN // LANES):
                cs = slice(cb * LANES, (cb + 1) * LANES)
                col = cbuf.at[cb]
                col[SUBLANES + r:SUBLANES + r + ROW_SUB, :] = res[:, cs]
                base = SUBLANES + r - (DN_CONV - 1)
                acc = [None] * SUBLANES
                for s in range(SUBLANES - 1 + DN_CONV):
                    xs = col[pl.ds(base + s, n, stride=SUBLANES), :]
                    for p in range(SUBLANES):
                        tap = s - p
                        if 0 <= tap < DN_CONV:
                            term = xs * cw_ref[tap:tap + 1, cs]
                            acc[p] = term if acc[p] is None else acc[p] + term
                for p in range(SUBLANES):
                    y = _silu(acc[p])
                    if l2_scale is not None:
                        y = y * (lax.rsqrt(jnp.sum(y * y, axis=-1, keepdims=True) + EPS) * l2_scale)
                    obuf.at[cb][pl.ds(r + p, n, stride=SUBLANES), :] = y
                out_ref[r:r + ROW_SUB, cs] = obuf[cb, r:r + ROW_SUB, :].astype(BF16)
        carry[slot] = cbuf[:, TM:TM + SUBLANES, :]
        if first:
            ba_ref[...] = _dot(hn_ref[...], wba_ref[...])

    pl.when(j == 0)(functools.partial(conv_tile, 0, DN_HEAD_DIM ** -0.5, first=True))
    pl.when(j == 1)(functools.partial(conv_tile, 1, 1.0))
    pl.when(j == 2)(functools.partial(conv_tile, 2, None))

    @pl.when(j >= 3)
    def _():
        for r in range(0, TM, ROW_SUB):
            out_ref[r:r + ROW_SUB, :] = _dot(hn_ref[r:r + ROW_SUB, :], w_ref[...]).astype(BF16)


def _inproj(x2d, mod6, g1, w_main, wba, conv_w, seq):
    m, d = x2d.shape
    n = w_main.shape[1]
    assert INPROJ_TN == DN_WIDTH and DN_HEAD_DIM == LANES, "q / k / v: one column tile each, one head per lane tile"
    tiles_per_seq = seq // INPROJ_TM
    n_conv = 3
    nb = INPROJ_TN // LANES
    return pl.pallas_call(
        functools.partial(_inproj_kernel, tiles_per_seq=tiles_per_seq),
        grid=(m // INPROJ_TM, n // INPROJ_TN),
        in_specs=[pl.BlockSpec((INPROJ_TM, d), lambda i, j: (i, 0)),
                  pl.BlockSpec((None, 6, d), lambda i, j: (i // tiles_per_seq, 0, 0)),
                  pl.BlockSpec((1, d), lambda i, j: (0, 0)),
                  pl.BlockSpec((d, INPROJ_TN), lambda i, j: (0, j)),
                  pl.BlockSpec((d, LANES), lambda i, j: (0, 0)),
                  pl.BlockSpec((DN_CONV, INPROJ_TN), lambda i, j: (0, jnp.minimum(j, n_conv - 1)))],
        out_specs=[pl.BlockSpec((INPROJ_TM, INPROJ_TN), lambda i, j: (i, j)),
                   pl.BlockSpec((INPROJ_TM, LANES), lambda i, j: (i, 0))],
        out_shape=[jax.ShapeDtypeStruct((m, n), BF16),
                   jax.ShapeDtypeStruct((m, LANES), F32)],
        scratch_shapes=[pltpu.VMEM((INPROJ_TM, d), BF16),
                        pltpu.VMEM((nb, INPROJ_TM + SUBLANES, LANES), F32),
                        pltpu.VMEM((nb, INPROJ_TM, LANES), F32),
                        pltpu.VMEM((n_conv, nb, SUBLANES, LANES), F32)],
        compiler_params=_params(("arbitrary", "arbitrary")),
        name="inproj",
    )(x2d, mod6, g1, w_main, wba, conv_w)


def _split3(x):
    h1 = x.astype(BF16)
    r1 = x - h1.astype(F32)
    h2 = r1.astype(BF16)
    h3 = (r1 - h2.astype(F32)).astype(BF16)
    return h1, h2, h3


def _deltanet_kernel(q_ref, k_ref, v_ref, z_ref, ba_ref, alog_ref, dtb_ref, ng_ref, out_ref, state):
    t = pl.program_id(1)
    T, C, H, DH = DN_T, DN_CHUNK, DN_HEADS, DN_HEAD_DIM
    NC = T // C

    @pl.when(t == 0)
    def _():
        state[...] = jnp.zeros_like(state)

    row = lax.broadcasted_iota(jnp.int32, (T, T), 0)
    col = lax.broadcasted_iota(jnp.int32, (T, T), 1)
    same = (row // C) == (col // C)
    lower = jnp.where(jnp.logical_and(same, col <= row), 1.0, 0.0).astype(BF16)
    upper = jnp.where(jnp.logical_and(same, row <= col), 1.0, 0.0).astype(BF16)

    def gates(rb):
        ba = ba_ref[rb]
        beta_all = _sigmoid(ba)
        sp_in = ba + dtb_ref[...]
        softplus = jnp.maximum(sp_in, 0.0) + jnp.log(1.0 + jnp.exp(-jnp.abs(sp_in)))
        g_all = -jnp.exp(alog_ref[...]) * softplus
        g1, g2, g3 = _split3(g_all)
        gc_col_all = _dot(lower, g1) + _dot(lower, g2) + _dot(lower, g3)
        g_rows = g_all.T[0:2 * H, :]
        r1, r2, r3 = _split3(g_rows)
        gc_rows = _dot(r1, upper) + _dot(r2, upper) + _dot(r3, upper)
        return beta_all, gc_col_all, gc_rows, jnp.exp(gc_rows)

    wrow = lax.broadcasted_iota(jnp.int32, (C, T), 0)
    wcol = lax.broadcasted_iota(jnp.int32, (C, T), 1) % C
    causal_w = wcol <= wrow
    strict_w = wcol < wrow
    first_half = lax.broadcasted_iota(jnp.int32, (C, LANES), 1) < C

    def widen(x):
        cols = []
        for p in range(NC // 2):
            ls = slice(p * LANES, (p + 1) * LANES)
            cols.append(jnp.where(first_half, x[2 * p * C:(2 * p + 1) * C, ls],
                                  x[(2 * p + 1) * C:(2 * p + 2) * C, ls]))
        return jnp.concatenate(cols, axis=1)

    same_b = jnp.where(same, 1.0, 0.0).astype(BF16)

    def blockdiag(xw):
        return jnp.concatenate([xw.astype(BF16)] * NC, axis=0) * same_b

    def head_prep(rb, h, gate_vals, done):
        beta_all, gc_col_all, gc_rows, eg_rows = gate_vals
        hs = slice(h * DH, (h + 1) * DH)
        qb = q_ref[rb, :, hs]
        q = qb.astype(F32)
        k = k_ref[rb, :, hs].astype(F32)
        v = v_ref[rb, :, hs].astype(F32)
        beta = jnp.broadcast_to(beta_all[:, h:h + 1], (T, DH))
        gcol = jnp.broadcast_to(gc_col_all[:, H + h:H + h + 1], (T, LANES))
        egcol = jnp.exp(gcol)
        grow = gc_rows[H + h:H + h + 1, :]
        gcol_w = widen(jnp.concatenate([gcol] * (NC // 2), axis=1))
        dec_w = jnp.exp(jnp.where(causal_w, gcol_w - grow, -1e30))

        kb = k * beta
        kT = k.T
        yield
        mm = _dot(jnp.concatenate([kb.astype(BF16), qb], axis=0), kT.astype(BF16))
        yield
        a_w = jnp.where(strict_w, widen(mm[:T]) * dec_w, 0.0)
        attn_w = widen(mm[T:]) * dec_w

        r_w = -a_w
        p_w = _dot(a_w.astype(BF16), blockdiag(a_w))
        yield
        n_sq = 5
        for s in range(n_sq):
            pbd = blockdiag(p_w)
            if s < n_sq - 1:
                rp = _dot(jnp.concatenate([r_w, p_w], axis=0).astype(BF16), pbd)
                r_w = r_w + p_w + rp[:C]
                p_w = rp[C:]
            else:
                r_w = r_w + p_w + _dot(r_w.astype(BF16), pbd)
            yield

        rhs = jnp.concatenate([v * beta, kb * egcol], axis=1)
        sol = rhs + _dot(blockdiag(r_w), rhs.astype(BF16))
        yield
        u0 = sol[:, :DH]
        qd = q * egcol
        wq = [jnp.concatenate([sol[c * C:(c + 1) * C, DH:], qd[c * C:(c + 1) * C]], axis=0).astype(BF16)
              for c in range(NC)]
        kd_t = kT * dec_w[C - 1:C, :]
        ak = [jnp.concatenate([attn_w[:, p * LANES:(p + 1) * LANES], kd_t[:, p * LANES:(p + 1) * LANES]],
                              axis=0).astype(BF16) for p in range(NC // 2)]
        eg = [eg_rows[H + h:H + h + 1, c * C + C - 1:c * C + C] for c in range(NC)]
        done[rb * H + h] = (u0, wq, ak, eg)

    def pair_diag(x1, x2):
        z = jnp.zeros_like(x1)
        return jnp.concatenate([jnp.concatenate([x1, z], axis=1), jnp.concatenate([z, x2], axis=1)], axis=0)

    def pair_recurrence(rb, hp, prep):
        hs = (2 * hp, 2 * hp + 1)
        pr = [prep[rb * H + h] for h in hs]
        st = [state[rb * H + h] for h in hs]
        outs = [[], []]
        for c in range(NC):
            res = _dot(jnp.concatenate([pr[0][1][c], pr[1][1][c]], axis=1),
                       pair_diag(st[0].astype(BF16), st[1].astype(BF16)))
            yield
            u_pairs = []
            for i in range(2):
                u = pr[i][0][c * C:(c + 1) * C] - res[:C, i * DH:(i + 1) * DH]
                zu = jnp.zeros_like(u)
                u_pairs.append(jnp.concatenate([u, zu] if c % 2 == 0 else [zu, u], axis=0).astype(BF16))
            res2 = _dot(jnp.concatenate([pr[0][2][c // 2], pr[1][2][c // 2]], axis=1),
                        pair_diag(u_pairs[0], u_pairs[1]))
            yield
            for i in range(2):
                ls = slice(i * DH, (i + 1) * DH)
                outs[i].append(res[C:, ls] + res2[:C, ls])
                st[i] = st[i] * pr[i][3][c] + res2[C:, ls]
        for i, h in enumerate(hs):
            state[rb * H + h] = st[i]
            o = jnp.concatenate(outs[i], axis=0)
            o = o * lax.rsqrt(jnp.mean(o * o, axis=-1, keepdims=True) + EPS) * ng_ref[...]
            zz = z_ref[rb, :, h * DH:(h + 1) * DH].astype(F32)
            out_ref[rb, :, h * DH:(h + 1) * DH] = (o * _silu(zz)).astype(BF16)

    def lockstep(gens):
        gens = list(gens)
        while gens:
            gens = [g for g in gens if next(g, True) is None]

    prep = [None] * (DN_ROWS * H)
    gate_vals = [gates(rb) for rb in range(DN_ROWS)]
    lockstep(head_prep(rb, h, gate_vals[rb], prep) for rb in range(DN_ROWS) for h in range(H))
    lockstep(pair_recurrence(rb, hp, prep) for rb in range(DN_ROWS) for hp in range(H // 2))


def _deltanet(proj, ba, a_log, dt_bias, norm_g, bsz, seq):
    nt = seq // DN_T
    w = DN_WIDTH
    pad = lambda p: jnp.zeros((1, LANES), F32).at[0, DN_HEADS:2 * DN_HEADS].set(p)
    proj3 = proj.reshape(bsz, seq, proj.shape[1])
    col = lambda c: pl.BlockSpec((DN_ROWS, DN_T, w), lambda b, t: (b, t, c))
    return pl.pallas_call(
        _deltanet_kernel,
        grid=(bsz // DN_ROWS, nt),
        in_specs=[col(0), col(1), col(2), col(COL_Z // w),
                  pl.BlockSpec((DN_ROWS, DN_T, LANES), lambda b, t: (b, t, 0)),
                  pl.BlockSpec((1, LANES), lambda b, t: (0, 0)),
                  pl.BlockSpec((1, LANES), lambda b, t: (0, 0)),
                  pl.BlockSpec((1, DN_HEAD_DIM), lambda b, t: (0, 0))],
        out_specs=pl.BlockSpec((DN_ROWS, DN_T, w), lambda b, t: (b, t, 0)),
        out_shape=jax.ShapeDtypeStruct((bsz, seq, w), BF16),
        scratch_shapes=[pltpu.VMEM((DN_ROWS * DN_HEADS, DN_HEAD_DIM, DN_HEAD_DIM), F32)],
        compiler_params=_params(("arbitrary", "arbitrary")),
        name="deltanet",
    )(proj3, proj3, proj3, proj3, ba.reshape(bsz, seq, LANES), pad(a_log), pad(dt_bias),
      norm_g.reshape(1, DN_HEAD_DIM)).reshape(bsz * seq, w)


def _conformer_kernel(val_ref, gl_ref, cw_ref, lg_ref, lb_ref, *rest, n_cast):
    out_ref = rest[n_cast]
    ubuf, cbuf = rest[-2:]
    for src, dst in zip(rest[:n_cast], rest[n_cast + 1:2 * n_cast + 1]):
        dst[...] = src[...].astype(BF16)
    t = pl.program_id(1)
    T = CF_T
    NB = CF_WIDTH // LANES

    @pl.when(t == 0)
    def _():
        ubuf[:, 0:CF_HALO, :] = jnp.zeros((NB, CF_HALO, LANES), F32)

    @pl.when(t > 0)
    def _():
        ubuf[:, 0:CF_HALO, :] = ubuf[:, T:T + CF_HALO, :]

    off = CF_HALO - (CF_KERNEL - 1)
    n = T // SUBLANES
    for cb in range(NB):
        cs = slice(cb * LANES, (cb + 1) * LANES)
        ucol = ubuf.at[cb]
        ucol[CF_HALO:CF_HALO + T, :] = val_ref[:, cs].astype(F32) * _sigmoid(gl_ref[:, cs].astype(F32))
        acc = [None] * SUBLANES
        for s in range(off, off + SUBLANES - 1 + CF_KERNEL):
            xs = ucol[pl.ds(s, n, stride=SUBLANES), :]
            for p in range(SUBLANES):
                j = s - off - p
                if 0 <= j < CF_KERNEL:
                    term = xs * cw_ref[j:j + 1, cs]
                    acc[p] = term if acc[p] is None else acc[p] + term
        for p in range(SUBLANES):
            cbuf.at[cb][pl.ds(p, n, stride=SUBLANES), :] = acc[p]

    y = jnp.concatenate([cbuf[cb] for cb in range(NB)], axis=1)
    mu = jnp.mean(y, axis=-1, keepdims=True)
    yc = y - mu
    var = jnp.mean(yc * yc, axis=-1, keepdims=True)
    yn = yc * lax.rsqrt(var + EPS) * lg_ref[...] + lb_ref[...]
    out_ref[...] = _silu(yn).astype(BF16)


def _conformer(proj, conv_w, ln_g, ln_b, bsz, seq, layer, weights):
    nt = seq // CF_T
    n_steps = bsz * nt
    tok = lambda b, t: b * nt + t
    bf16_sublanes = 2 * SUBLANES
    w_in_specs, w_out_specs, w_out_shapes = [], [], []
    for w in weights:
        _, rows, cols = w.shape
        share = 1
        while rows * share % n_steps or (rows * share // n_steps) % bf16_sublanes:
            share *= 2
        slab = rows * share // n_steps
        w_in_specs.append(pl.BlockSpec((None, slab, cols), lambda b, t, share=share: (layer, tok(b, t) // share, 0)))
        w_out_specs.append(pl.BlockSpec((slab, cols), lambda b, t, share=share: (tok(b, t) // share, 0)))
        w_out_shapes.append(jax.ShapeDtypeStruct((rows, cols), BF16))
    return pl.pallas_call(
        functools.partial(_conformer_kernel, n_cast=len(weights)),
        grid=(bsz, nt),
        in_specs=[pl.BlockSpec((CF_T, CF_WIDTH), lambda b, t: (tok(b, t), COL_GLU // CF_WIDTH)),
                  pl.BlockSpec((CF_T, CF_WIDTH), lambda b, t: (tok(b, t), COL_GLU // CF_WIDTH + 1)),
                  pl.BlockSpec((CF_KERNEL, CF_WIDTH), lambda b, t: (0, 0)),
                  pl.BlockSpec((1, CF_WIDTH), lambda b, t: (0, 0)),
                  pl.BlockSpec((1, CF_WIDTH), lambda b, t: (0, 0))] + w_in_specs,
        out_specs=[pl.BlockSpec((CF_T, CF_WIDTH), lambda b, t: (tok(b, t), 0))] + w_out_specs,
        out_shape=[jax.ShapeDtypeStruct((bsz * seq, CF_WIDTH), BF16)] + w_out_shapes,
        scratch_shapes=[pltpu.VMEM((CF_WIDTH // LANES, CF_T + CF_HALO, LANES), F32),
                        pltpu.VMEM((CF_WIDTH // LANES, CF_T, LANES), F32)],
        compiler_params=_params(("arbitrary", "arbitrary")),
        name="conformer",
    )(proj, proj, conv_w, ln_g.reshape(1, CF_WIDTH), ln_b.reshape(1, CF_WIDTH), *weights)


def _merge_kernel(o_ref, u_ref, ga_ref, gb_ref, x_ref, mod_ref, g2_ref, wdo_ref, wco_ref, wout_ref,
                  x1_ref, hn2_ref):
    merged = []
    for r in range(0, MERGE_TM, ROW_SUB):
        rs = slice(r, r + ROW_SUB)
        branch_a = _dot(o_ref[rs, :], wdo_ref[...])
        branch_b = _dot(u_ref[rs, :], wco_ref[...])
        merged.append((_sigmoid(ga_ref[rs, :].astype(F32)) * branch_a
                       + _sigmoid(gb_ref[rs, :].astype(F32)) * branch_b).astype(BF16))
    for idx, r in enumerate(range(0, MERGE_TM, ROW_SUB)):
        rs = slice(r, r + ROW_SUB)
        mix = _dot(merged[idx], wout_ref[...])
        x1 = x_ref[rs, :] + mod_ref[2:3, :] * mix
        x1_ref[rs, :] = x1
        hn2 = _modulated_rmsnorm(x1, g2_ref[...], mod_ref[3:4, :], mod_ref[4:5, :])
        hn2_ref[rs, :] = hn2.astype(BF16)


def _merge(o_gated, u_act, proj, x2d, mod6, g2, w_dn_o, w_cf_o, w_out, seq):
    m, d = x2d.shape
    tiles_per_seq = seq // MERGE_TM
    const = lambda shape: pl.BlockSpec(shape, lambda i: (0, 0), pipeline_mode=pl.Buffered(1))
    return pl.pallas_call(
        _merge_kernel,
        grid=(m // MERGE_TM,),
        in_specs=[pl.BlockSpec((MERGE_TM, DN_WIDTH), lambda i: (i, 0)),
                  pl.BlockSpec((MERGE_TM, CF_WIDTH), lambda i: (i, 0)),
                  pl.BlockSpec((MERGE_TM, d), lambda i: (i, COL_GATE_A // d)),
                  pl.BlockSpec((MERGE_TM, d), lambda i: (i, COL_GATE_B // d)),
                  pl.BlockSpec((MERGE_TM, d), lambda i: (i, 0)),
                  pl.BlockSpec((None, 6, d), lambda i: (i // tiles_per_seq, 0, 0)),
                  pl.BlockSpec((1, d), lambda i: (0, 0)),
                  const((DN_WIDTH, d)),
                  const((CF_WIDTH, d)),
                  const((d, d))],
        out_specs=[pl.BlockSpec((MERGE_TM, d), lambda i: (i, 0)),
                   pl.BlockSpec((MERGE_TM, d), lambda i: (i, 0))],
        out_shape=[jax.ShapeDtypeStruct((m, d), F32),
                   jax.ShapeDtypeStruct((m, d), BF16)],
        compiler_params=_params(("arbitrary",)),
        name="merge",
    )(o_gated, u_act, proj, proj, x2d, mod6, g2, w_dn_o, w_cf_o, w_out)


def _ffn_kernel(hn_ref, x1_ref, mod_ref, gf_ref, wg_ref, wu_ref, cw_ref, wd_ref, out_ref,
                gbuf, carry, *, tiles_per_seq):
    i = pl.program_id(0)
    j = pl.program_id(1)
    nf = pl.num_programs(1)
    TM = FFN_TM

    @pl.when(j == 0)
    def _():
        out_ref[...] = jnp.zeros_like(out_ref)

    @pl.when(i == 0)
    def _():
        carry[j] = jnp.zeros((SUBLANES, FFN_TF), F32)

    keep = (i % tiles_per_seq != 0).astype(F32)
    gbuf[0:SUBLANES, :] = carry[j] * keep

    hs = []
    for r in range(0, TM, ROW_SUB):
        hn = hn_ref[r:r + ROW_SUB, :]
        gbuf[SUBLANES + r:SUBLANES + r + ROW_SUB, :] = _dot(hn, wg_ref[...])
        up = _dot(hn, wu_ref[...])
        acc = None
        for tap in range(FFN_CONV):
            o = SUBLANES + r - (FFN_CONV - 1) + tap
            term = gbuf[o:o + ROW_SUB, :] * cw_ref[tap:tap + 1, :]
            acc = term if acc is None else acc + term
        hs.append((_silu(acc) * up).astype(BF16))
    for idx, r in enumerate(range(0, TM, ROW_SUB)):
        out_ref[r:r + ROW_SUB, :] += _dot(hs[idx], wd_ref[...])

    carry[j] = gbuf[TM:TM + SUBLANES, :]

    @pl.when(j == nf - 1)
    def _():
        x2 = x1_ref[...] + mod_ref[5:6, :] * out_ref[...]
        ms = jnp.mean(x2 * x2, axis=-1, keepdims=True)
        out_ref[...] = x2 * lax.rsqrt(ms + EPS) * gf_ref[...]


def _ffn(hn2, x1, mod6, gf, w_up, conv_w, w_down, seq):
    m, d = x1.shape
    nf = FFN_DIM // FFN_TF
    tiles_per_seq = seq // FFN_TM
    return pl.pallas_call(
        functools.partial(_ffn_kernel, tiles_per_seq=tiles_per_seq),
        grid=(m // FFN_TM, nf),
        in_specs=[pl.BlockSpec((FFN_TM, d), lambda i, j: (i, 0)),
                  pl.BlockSpec((FFN_TM, d), lambda i, j: (i, 0)),
                  pl.BlockSpec((None, 6, d), lambda i, j: (i // tiles_per_seq, 0, 0)),
                  pl.BlockSpec((1, d), lambda i, j: (0, 0)),
                  pl.BlockSpec((d, FFN_TF), lambda i, j: (0, j)),
                  pl.BlockSpec((d, FFN_TF), lambda i, j: (0, nf + j)),
                  pl.BlockSpec((FFN_CONV, FFN_TF), lambda i, j: (0, j)),
                  pl.BlockSpec((FFN_TF, d), lambda i, j: (j, 0))],
        out_specs=pl.BlockSpec((FFN_TM, d), lambda i, j: (i, 0)),
        out_shape=jax.ShapeDtypeStruct((m, d), F32),
        scratch_shapes=[pltpu.VMEM((FFN_TM + SUBLANES, FFN_TF), F32),
                        pltpu.VMEM((nf, SUBLANES, FFN_TF), F32)],
        compiler_params=_params(("arbitrary", "arbitrary")),
        name="ffn",
    )(hn2, x1, mod6, gf, w_up, w_up, conv_w, w_down)


def kernel(x, c, w_ada, b_ada, norm1_g, w_in, dn_conv_w, dn_a_log, dn_dt_bias, dn_norm_g, dn_w_o,
           cf_conv_w, cf_ln_g, cf_ln_b, cf_w_o, w_out, norm2_g, ffn_w_up, ffn_conv_w, ffn_w_down,
           final_norm_g):
    bsz, seq, d = x.shape
    assert w_ada.shape[0] == 1, "single-layer block"
    x2d = x.reshape(bsz * seq, d)
    for l in range(1):
        mod6 = _ada(c, w_ada[l], b_ada[l]).reshape(bsz, 6, d)
        w_in_t = jnp.transpose(w_in[l])
        w_main, wba = _repack_w_in(w_in_t)
        proj, ba = _inproj(x2d, mod6, norm1_g[l].reshape(1, d), w_main, wba, dn_conv_w[l], seq)
        o_gated = _deltanet(proj, ba, dn_a_log[l], dn_dt_bias[l], dn_norm_g[l], bsz, seq)
        u_act, w_dn_o, w_cf_o, w_out_b, w_up, w_down = _conformer(
            proj, cf_conv_w[l], cf_ln_g[l], cf_ln_b[l], bsz, seq, l, [dn_w_o, cf_w_o, w_out, ffn_w_up, ffn_w_down])
        x1, hn2 = _merge(o_gated, u_act, proj, x2d, mod6, norm2_g[l].reshape(1, d), w_dn_o, w_cf_o, w_out_b, seq)
        x2d = _ffn(hn2, x1, mod6, final_norm_g.reshape(1, d), w_up, ffn_conv_w[l], w_down, seq)
    return x2d.reshape(bsz, seq, d)
```

```python
import functools

import jax
import jax.numpy as jnp
from jax import lax
from jax.experimental import pallas as pl
from jax.experimental.pallas import tpu as pltpu

F32 = jnp.float32
BF16 = jnp.bfloat16

D_MODEL = 2048
DN_HEADS = 8
DN_HEAD_DIM = 128
DN_WIDTH = DN_HEADS * DN_HEAD_DIM
DN_CONV = 4
DN_CHUNK = 64
CF_WIDTH = 1024
CF_KERNEL = 31
FFN_DIM = 5632
FFN_CONV = 3
EPS = 1e-6

LANES = 128
SUBLANES = 8
VMEM_LIMIT = 60000 * 1024

COL_Z = 3 * DN_WIDTH
COL_GLU = COL_Z + DN_WIDTH
COL_GATE_A = COL_GLU + 2 * CF_WIDTH
COL_GATE_B = COL_GATE_A + D_MODEL
N_MAIN = COL_GATE_B + D_MODEL

ROW_SUB = 256
REPACK_COLS = 512
ADA_TN = 1024
INPROJ_TM = 1024
INPROJ_TN = 1024
DN_T = 256
DN_ROWS = 2
CF_T = 256
CF_HALO = 32
MERGE_TM = 512
FFN_TM = 512
FFN_TF = 512


def _dot(a, b):
    return jnp.dot(a, b, preferred_element_type=F32)


def _sigmoid(x):
    return 0.5 * jnp.tanh(0.5 * x) + 0.5


def _silu(x):
    h = 0.5 * x
    return h * jnp.tanh(h) + h


def _params(sem):
    return pltpu.CompilerParams(dimension_semantics=sem, vmem_limit_bytes=VMEM_LIMIT)


def _ada_kernel(c_ref, w_ref, b_ref, o_ref):
    c = c_ref[...]
    cact = _silu(c).astype(BF16)
    o_ref[...] = _dot(cact, w_ref[...].astype(BF16)) + b_ref[...]


def _ada(c, w, b):
    bsz, d = c.shape
    n = w.shape[1]
    return pl.pallas_call(
        _ada_kernel,
        grid=(n // ADA_TN,),
        in_specs=[pl.BlockSpec((bsz, d), lambda j: (0, 0)),
                  pl.BlockSpec((d, ADA_TN), lambda j: (0, j)),
                  pl.BlockSpec((1, ADA_TN), lambda j: (0, j))],
        out_specs=pl.BlockSpec((bsz, ADA_TN), lambda j: (0, j)),
        out_shape=jax.ShapeDtypeStruct((bsz, n), F32),
        compiler_params=_params(("arbitrary",)),
        name="ada",
    )(c, w, b.reshape(1, n))


def _repack_kernel(wt_ref, wt_ba_ref, main_ref, ba_ref):
    main_ref[...] = wt_ref[...].T.astype(BF16)

    @pl.when(pl.program_id(0) == 0)
    def _():
        lane = lax.broadcasted_iota(jnp.int32, ba_ref.shape, 1)
        ba_ref[...] = jnp.where(lane < 2 * DN_HEADS, wt_ba_ref[...].T, 0.0).astype(BF16)


def _repack_w_in(w_t):
    n, d = w_t.shape
    lo, skip = COL_GLU, 2 * DN_HEADS
    assert lo % REPACK_COLS == 0 and skip % SUBLANES == 0
    return pl.pallas_call(
        _repack_kernel,
        grid=(N_MAIN // REPACK_COLS,),
        in_specs=[pl.BlockSpec((pl.Element(REPACK_COLS), pl.Element(d)),
                               lambda c: ((c * (REPACK_COLS // SUBLANES)
                                           + jnp.where(c >= lo // REPACK_COLS, skip // SUBLANES, 0)) * SUBLANES, 0)),
                  pl.BlockSpec((LANES, d), lambda c: (lo // LANES, 0))],
        out_specs=[pl.BlockSpec((d, REPACK_COLS), lambda c: (0, c)),
                   pl.BlockSpec((d, LANES), lambda c: (0, 0))],
        out_shape=[jax.ShapeDtypeStruct((d, N_MAIN), BF16),
                   jax.ShapeDtypeStruct((d, LANES), BF16)],
        compiler_params=_params(("arbitrary",)),
        name="repack",
    )(w_t, w_t)


def _modulated_rmsnorm(x, g, shift, scale):
    ms = jnp.mean(x * x, axis=-1, keepdims=True)
    y = x * lax.rsqrt(ms + EPS) * g
    return y * (1.0 + scale) + shift


def _inproj_kernel(x_ref, mod_ref, g_ref, w_ref, wba_ref, cw_ref, out_ref, ba_ref, hn_ref, cbuf, obuf, carry,
                   *, tiles_per_seq):
    i = pl.program_id(0)
    j = pl.program_id(1)
    TM = INPROJ_TM

    @pl.when(jnp.logical_and(i == 0, j == 0))
    def _():
        carry[...] = jnp.zeros_like(carry)

    def conv_tile(slot, l2_scale, first=False):
        keep = (i % tiles_per_seq != 0).astype(F32)
        cbuf[:, 0:SUBLANES, :] = carry[slot] * keep
        n = ROW_SUB // SUBLANES
        for r in range(0, TM, ROW_SUB):
            if first:
                hn = _modulated_rmsnorm(x_ref[r:r + ROW_SUB, :], g_ref[...], mod_ref[0:1, :], mod_ref[1:2, :])
                hn_ref[r:r + ROW_SUB, :] = hn.astype(BF16)
            res = _dot(hn_ref[r:r + ROW_SUB, :], w_ref[...])
            for cb in range(INPROJ_TN // LANES):
                cs = slice(cb * LANES, (cb + 1) * LANES)
                col = cbuf.at[cb]
                col[SUBLANES + r:SUBLANES + r + ROW_SUB, :] = res[:, cs]
                base = SUBLANES + r - (DN_CONV - 1)
                acc = [None] * SUBLANES
                for s in range(SUBLANES - 1 + DN_CONV):
                    xs = col[pl.ds(base + s, n, stride=SUBLANES), :]
                    for p in range(SUBLANES):
                        tap = s - p
                        if 0 <= tap < DN_CONV:
                            term = xs * cw_ref[tap:tap + 1, cs]
                            acc[p] = term if acc[p] is None else acc[p] + term
                for p in range(SUBLANES):
                    y = _silu(acc[p])
                    if l2_scale is not None:
                        y = y * (lax.rsqrt(jnp.sum(y * y, axis=-1, keepdims=True) + EPS) * l2_scale)
                    obuf.at[cb][pl.ds(r + p, n, stride=SUBLANES), :] = y
                out_ref[r:r + ROW_SUB, cs] = obuf[cb, r:r + ROW_SUB, :].astype(BF16)
        carry[slot] = cbuf[:, TM:TM + SUBLANES, :]
        if first:
            ba_ref[...] = _dot(hn_ref[...], wba_ref[...])

    pl.when(j == 0)(functools.partial(conv_tile, 0, DN_HEAD_DIM ** -0.5, first=True))
    pl.when(j == 1)(functools.partial(conv_tile, 1, 1.0))
    pl.when(j == 2)(functools.partial(conv_tile, 2, None))

    @pl.when(j >= 3)
    def _():
        for r in range(0, TM, ROW_SUB):
            out_ref[r:r + ROW_SUB, :] = _dot(hn_ref[r:r + ROW_SUB, :], w_ref[...]).astype(BF16)


def _inproj(x2d, mod6, g1, w_main, wba, conv_w, seq):
    m, d = x2d.shape
    n = w_main.shape[1]
    assert INPROJ_TN == DN_WIDTH and DN_HEAD_DIM == LANES, "q / k / v: one column tile each, one head per lane tile"
    tiles_per_seq = seq // INPROJ_TM
    n_conv = 3
    nb = INPROJ_TN // LANES
    return pl.pallas_call(
        functools.partial(_inproj_kernel, tiles_per_seq=tiles_per_seq),
        grid=(m // INPROJ_TM, n // INPROJ_TN),
        in_specs=[pl.BlockSpec((INPROJ_TM, d), lambda i, j: (i, 0)),
                  pl.BlockSpec((None, 6, d), lambda i, j: (i // tiles_per_seq, 0, 0)),
                  pl.BlockSpec((1, d), lambda i, j: (0, 0)),
                  pl.BlockSpec((d, INPROJ_TN), lambda i, j: (0, j)),
                  pl.BlockSpec((d, LANES), lambda i, j: (0, 0)),
                  pl.BlockSpec((DN_CONV, INPROJ_TN), lambda i, j: (0, jnp.minimum(j, n_conv - 1)))],
        out_specs=[pl.BlockSpec((INPROJ_TM, INPROJ_TN), lambda i, j: (i, j)),
                   pl.BlockSpec((INPROJ_TM, LANES), lambda i, j: (i, 0))],
        out_shape=[jax.ShapeDtypeStruct((m, n), BF16),
                   jax.ShapeDtypeStruct((m, LANES), F32)],
        scratch_shapes=[pltpu.VMEM((INPROJ_TM, d), BF16),
                        pltpu.VMEM((nb, INPROJ_TM + SUBLANES, LANES), F32),
                        pltpu.VMEM((nb, INPROJ_TM, LANES), F32),
                        pltpu.VMEM((n_conv, nb, SUBLANES, LANES), F32)],
        compiler_params=_params(("arbitrary", "arbitrary")),
        name="inproj",
    )(x2d, mod6, g1, w_main, wba, conv_w)


def _split3(x):
    h1 = x.astype(BF16)
    r1 = x - h1.astype(F32)
    h2 = r1.astype(BF16)
    h3 = (r1 - h2.astype(F32)).astype(BF16)
    return h1, h2, h3


def _deltanet_kernel(q_ref, k_ref, v_ref, z_ref, ba_ref, alog_ref, dtb_ref, ng_ref, out_ref, state):
    t = pl.program_id(1)
    T, C, H, DH = DN_T, DN_CHUNK, DN_HEADS, DN_HEAD_DIM
    NC = T // C

    @pl.when(t == 0)
    def _():
        state[...] = jnp.zeros_like(state)

    row = lax.broadcasted_iota(jnp.int32, (T, T), 0)
    col = lax.broadcasted_iota(jnp.int32, (T, T), 1)
    same = (row // C) == (col // C)
    lower = jnp.where(jnp.logical_and(same, col <= row), 1.0, 0.0).astype(BF16)
    upper = jnp.where(jnp.logical_and(same, row <= col), 1.0, 0.0).astype(BF16)

    def gates(rb):
        ba = ba_ref[rb]
        beta_all = _sigmoid(ba)
        sp_in = ba + dtb_ref[...]
        softplus = jnp.maximum(sp_in, 0.0) + jnp.log(1.0 + jnp.exp(-jnp.abs(sp_in)))
        g_all = -jnp.exp(alog_ref[...]) * softplus
        g1, g2, g3 = _split3(g_all)
        gc_col_all = _dot(lower, g1) + _dot(lower, g2) + _dot(lower, g3)
        g_rows = g_all.T[0:2 * H, :]
        r1, r2, r3 = _split3(g_rows)
        gc_rows = _dot(r1, upper) + _dot(r2, upper) + _dot(r3, upper)
        return beta_all, gc_col_all, gc_rows, jnp.exp(gc_rows)

    wrow = lax.broadcasted_iota(jnp.int32, (C, T), 0)
    wcol = lax.broadcasted_iota(jnp.int32, (C, T), 1) % C
    causal_w = wcol <= wrow
    strict_w = wcol < wrow
    first_half = lax.broadcasted_iota(jnp.int32, (C, LANES), 1) < C

    def widen(x):
        cols = []
        for p in range(NC // 2):
            ls = slice(p * LANES, (p + 1) * LANES)
            cols.append(jnp.where(first_half, x[2 * p * C:(2 * p + 1) * C, ls],
                                  x[(2 * p + 1) * C:(2 * p + 2) * C, ls]))
        return jnp.concatenate(cols, axis=1)

    same_b = jnp.where(same, 1.0, 0.0).astype(BF16)

    def blockdiag(xw):
        return jnp.concatenate([xw.astype(BF16)] * NC, axis=0) * same_b

    def head_prep(rb, h, gate_vals, done):
        beta_all, gc_col_all, gc_rows, eg_rows = gate_vals
        hs = slice(h * DH, (h + 1) * DH)
        qb = q_ref[rb, :, hs]
        q = qb.astype(F32)
        k = k_ref[rb, :, hs].astype(F32)
        v = v_ref[rb, :, hs].astype(F32)
        beta = jnp.broadcast_to(beta_all[:, h:h + 1], (T, DH))
        gcol = jnp.broadcast_to(gc_col_all[:, H + h:H + h + 1], (T, LANES))
        egcol = jnp.exp(gcol)
        grow = gc_rows[H + h:H + h + 1, :]
        gcol_w = widen(jnp.concatenate([gcol] * (NC // 2), axis=1))
        dec_w = jnp.exp(jnp.where(causal_w, gcol_w - grow, -1e30))

        kb = k * beta
        kT = k.T
        yield
        mm = _dot(jnp.concatenate([kb.astype(BF16), qb], axis=0), kT.astype(BF16))
        yield
        a_w = jnp.where(strict_w, widen(mm[:T]) * dec_w, 0.0)
        attn_w = widen(mm[T:]) * dec_w

        r_w = -a_w
        p_w = _dot(a_w.astype(BF16), blockdiag(a_w))
        yield
        n_sq = 5
        for s in range(n_sq):
            pbd = blockdiag(p_w)
            if s < n_sq - 1:
                rp = _dot(jnp.concatenate([r_w, p_w], axis=0).astype(BF16), pbd)
                r_w = r_w + p_w + rp[:C]
                p_w = rp[C:]
            else:
                r_w = r_w + p_w + _dot(r_w.astype(BF16), pbd)
            yield

        rhs = jnp.concatenate([v * beta, kb * egcol], axis=1)
        sol = rhs + _dot(blockdiag(r_w), rhs.astype(BF16))
        yield
        u0 = sol[:, :DH]
        qd = q * egcol
        wq = [jnp.concatenate([sol[c * C:(c + 1) * C, DH:], qd[c * C:(c + 1) * C]], axis=0).astype(BF16)
              for c in range(NC)]
        kd_t = kT * dec_w[C - 1:C, :]
        ak = [jnp.concatenate([attn_w[:, p * LANES:(p + 1) * LANES], kd_t[:, p * LANES:(p + 1) * LANES]],
                              axis=0).astype(BF16) for p in range(NC // 2)]
        eg = [eg_rows[H + h:H + h + 1, c * C + C - 1:c * C + C] for c in range(NC)]
        done[rb * H + h] = (u0, wq, ak, eg)

    def pair_diag(x1, x2):
        z = jnp.zeros_like(x1)
        return jnp.concatenate([jnp.concatenate([x1, z], axis=1), jnp.concatenate([z, x2], axis=1)], axis=0)

    def pair_recurrence(rb, hp, prep):
        hs = (2 * hp, 2 * hp + 1)
        pr = [prep[rb * H + h] for h in hs]
        st = [state[rb * H + h] for h in hs]
        outs = [[], []]
        for c in range(NC):
            res = _dot(jnp.concatenate([pr[0][1][c], pr[1][1][c]], axis=1),
                       pair_diag(st[0].astype(BF16), st[1].astype(BF16)))
            yield
            u_pairs = []
            for i in range(2):
                u = pr[i][0][c * C:(c + 1) * C] - res[:C, i * DH:(i + 1) * DH]
                zu = jnp.zeros_like(u)
                u_pairs.append(jnp.concatenate([u, zu] if c % 2 == 0 else [zu, u], axis=0).astype(BF16))
            res2 = _dot(jnp.concatenate([pr[0][2][c // 2], pr[1][2][c // 2]], axis=1),
                        pair_diag(u_pairs[0], u_pairs[1]))
            yield
            for i in range(2):
                ls = slice(i * DH, (i + 1) * DH)
                outs[i].append(res[C:, ls] + res2[:C, ls])
                st[i] = st[i] * pr[i][3][c] + res2[C:, ls]
        for i, h in enumerate(hs):
            state[rb * H + h] = st[i]
            o = jnp.concatenate(outs[i], axis=0)
            o = o * lax.rsqrt(jnp.mean(o * o, axis=-1, keepdims=True) + EPS) * ng_ref[...]
            zz = z_ref[rb, :, h * DH:(h + 1) * DH].astype(F32)
            out_ref[rb, :, h * DH:(h + 1) * DH] = (o * _silu(zz)).astype(BF16)

    def lockstep(gens):
        gens = list(gens)
        while gens:
            gens = [g for g in gens if next(g, True) is None]

    prep = [None] * (DN_ROWS * H)
    gate_vals = [gates(rb) for rb in range(DN_ROWS)]
    lockstep(head_prep(rb, h, gate_vals[rb], prep) for rb in range(DN_ROWS) for h in range(H))
    lockstep(pair_recurrence(rb, hp, prep) for rb in range(DN_ROWS) for hp in range(H // 2))


def _deltanet(proj, ba, a_log, dt_bias, norm_g, bsz, seq):
    nt = seq // DN_T
    w = DN_WIDTH
    pad = lambda p: jnp.zeros((1, LANES), F32).at[0, DN_HEADS:2 * DN_HEADS].set(p)
    proj3 = proj.reshape(bsz, seq, proj.shape[1])
    col = lambda c: pl.BlockSpec((DN_ROWS, DN_T, w), lambda b, t: (b, t, c))
    return pl.pallas_call(
        _deltanet_kernel,
        grid=(bsz // DN_ROWS, nt),
        in_specs=[col(0), col(1), col(2), col(COL_Z // w),
                  pl.BlockSpec((DN_ROWS, DN_T, LANES), lambda b, t: (b, t, 0)),
                  pl.BlockSpec((1, LANES), lambda b, t: (0, 0)),
                  pl.BlockSpec((1, LANES), lambda b, t: (0, 0)),
                  pl.BlockSpec((1, DN_HEAD_DIM), lambda b, t: (0, 0))],
        out_specs=pl.BlockSpec((DN_ROWS, DN_T, w), lambda b, t: (b, t, 0)),
        out_shape=jax.ShapeDtypeStruct((bsz, seq, w), BF16),
        scratch_shapes=[pltpu.VMEM((DN_ROWS * DN_HEADS, DN_HEAD_DIM, DN_HEAD_DIM), F32)],
        compiler_params=_params(("arbitrary", "arbitrary")),
        name="deltanet",
    )(proj3, proj3, proj3, proj3, ba.reshape(bsz, seq, LANES), pad(a_log), pad(dt_bias),
      norm_g.reshape(1, DN_HEAD_DIM)).reshape(bsz * seq, w)


def _conformer_kernel(val_ref, gl_ref, cw_ref, lg_ref, lb_ref, *rest, n_cast):
    out_ref = rest[n_cast]
    ubuf, cbuf = rest[-2:]
    for src, dst in zip(rest[:n_cast], rest[n_cast + 1:2 * n_cast + 1]):
        dst[...] = src[...].astype(BF16)
    t = pl.program_id(1)
    T = CF_T
    NB = CF_WIDTH // LANES

    @pl.when(t == 0)
    def _():
        ubuf[:, 0:CF_HALO, :] = jnp.zeros((NB, CF_HALO, LANES), F32)

    @pl.when(t > 0)
    def _():
        ubuf[:, 0:CF_HALO, :] = ubuf[:, T:T + CF_HALO, :]

    off = CF_HALO - (CF_KERNEL - 1)
    n = T // SUBLANES
    for cb in range(NB):
        cs = slice(cb * LANES, (cb + 1) * LANES)
        ucol = ubuf.at[cb]
        ucol[CF_HALO:CF_HALO + T, :] = val_ref[:, cs].astype(F32) * _sigmoid(gl_ref[:, cs].astype(F32))
        acc = [None] * SUBLANES
        for s in range(off, off + SUBLANES - 1 + CF_KERNEL):
            xs = ucol[pl.ds(s, n, stride=SUBLANES), :]
            for p in range(SUBLANES):
                j = s - off - p
                if 0 <= j < CF_KERNEL:
                    term = xs * cw_ref[j:j + 1, cs]
                    acc[p] = term if acc[p] is None else acc[p] + term
        for p in range(SUBLANES):
            cbuf.at[cb][pl.ds(p, n, stride=SUBLANES), :] = acc[p]

    y = jnp.concatenate([cbuf[cb] for cb in range(NB)], axis=1)
    mu = jnp.mean(y, axis=-1, keepdims=True)
    yc = y - mu
    var = jnp.mean(yc * yc, axis=-1, keepdims=True)
    yn = yc * lax.rsqrt(var + EPS) * lg_ref[...] + lb_ref[...]
    out_ref[...] = _silu(yn).astype(BF16)


def _conformer(proj, conv_w, ln_g, ln_b, bsz, seq, layer, weights):
    nt = seq // CF_T
    n_steps = bsz * nt
    tok = lambda b, t: b * nt + t
    bf16_sublanes = 2 * SUBLANES
    w_in_specs, w_out_specs, w_out_shapes = [], [], []
    for w in weights:
        _, rows, cols = w.shape
        share = 1
        while rows * share % n_steps or (rows * share // n_steps) % bf16_sublanes:
            share *= 2
        slab = rows * share // n_steps
        w_in_specs.append(pl.BlockSpec((None, slab, cols), lambda b, t, share=share: (layer, tok(b, t) // share, 0)))
        w_out_specs.append(pl.BlockSpec((slab, cols), lambda b, t, share=share: (tok(b, t) // share, 0)))
        w_out_shapes.append(jax.ShapeDtypeStruct((rows, cols), BF16))
    return pl.pallas_call(
        functools.partial(_conformer_kernel, n_cast=len(weights)),
        grid=(bsz, nt),
        in_specs=[pl.BlockSpec((CF_T, CF_WIDTH), lambda b, t: (tok(b, t), COL_GLU // CF_WIDTH)),
                  pl.BlockSpec((CF_T, CF_WIDTH), lambda b, t: (tok(b, t), COL_GLU // CF_WIDTH + 1)),
                  pl.BlockSpec((CF_KERNEL, CF_WIDTH), lambda b, t: (0, 0)),
                  pl.BlockSpec((1, CF_WIDTH), lambda b, t: (0, 0)),
                  pl.BlockSpec((1, CF_WIDTH), lambda b, t: (0, 0))] + w_in_specs,
        out_specs=[pl.BlockSpec((CF_T, CF_WIDTH), lambda b, t: (tok(b, t), 0))] + w_out_specs,
        out_shape=[jax.ShapeDtypeStruct((bsz * seq, CF_WIDTH), BF16)] + w_out_shapes,
        scratch_shapes=[pltpu.VMEM((CF_WIDTH // LANES, CF_T + CF_HALO, LANES), F32),
                        pltpu.VMEM((CF_WIDTH // LANES, CF_T, LANES), F32)],
        compiler_params=_params(("arbitrary", "arbitrary")),
        name="conformer",
    )(proj, proj, conv_w, ln_g.reshape(1, CF_WIDTH), ln_b.reshape(1, CF_WIDTH), *weights)


def _merge_kernel(o_ref, u_ref, ga_ref, gb_ref, x_ref, mod_ref, g2_ref, wdo_ref, wco_ref, wout_ref,
                  x1_ref, hn2_ref):
    merged = []
    for r in range(0, MERGE_TM, ROW_SUB):
        rs = slice(r, r + ROW_SUB)
        branch_a = _dot(o_ref[rs, :], wdo_ref[...])
        branch_b = _dot(u_ref[rs, :], wco_ref[...])
        merged.append((_sigmoid(ga_ref[rs, :].astype(F32)) * branch_a
                       + _sigmoid(gb_ref[rs, :].astype(F32)) * branch_b).astype(BF16))
    for idx, r in enumerate(range(0, MERGE_TM, ROW_SUB)):
        rs = slice(r, r + ROW_SUB)
        mix = _dot(merged[idx], wout_ref[...])
        x1 = x_ref[rs, :] + mod_ref[2:3, :] * mix
        x1_ref[rs, :] = x1
        hn2 = _modulated_rmsnorm(x1, g2_ref[...], mod_ref[3:4, :], mod_ref[4:5, :])
        hn2_ref[rs, :] = hn2.astype(BF16)


def _merge(o_gated, u_act, proj, x2d, mod6, g2, w_dn_o, w_cf_o, w_out, seq):
    m, d = x2d.shape
    tiles_per_seq = seq // MERGE_TM
    const = lambda shape: pl.BlockSpec(shape, lambda i: (0, 0), pipeline_mode=pl.Buffered(1))
    return pl.pallas_call(
        _merge_kernel,
        grid=(m // MERGE_TM,),
        in_specs=[pl.BlockSpec((MERGE_TM, DN_WIDTH), lambda i: (i, 0)),
                  pl.BlockSpec((MERGE_TM, CF_WIDTH), lambda i: (i, 0)),
                  pl.BlockSpec((MERGE_TM, d), lambda i: (i, COL_GATE_A // d)),
                  pl.BlockSpec((MERGE_TM, d), lambda i: (i, COL_GATE_B // d)),
                  pl.BlockSpec((MERGE_TM, d), lambda i: (i, 0)),
                  pl.BlockSpec((None, 6, d), lambda i: (i // tiles_per_seq, 0, 0)),
                  pl.BlockSpec((1, d), lambda i: (0, 0)),
                  const((DN_WIDTH, d)),
                  const((CF_WIDTH, d)),
                  const((d, d))],
        out_specs=[pl.BlockSpec((MERGE_TM, d), lambda i: (i, 0)),
                   pl.BlockSpec((MERGE_TM, d), lambda i: (i, 0))],
        out_shape=[jax.ShapeDtypeStruct((m, d), F32),
                   jax.ShapeDtypeStruct((m, d), BF16)],
        compiler_params=_params(("arbitrary",)),
        name="merge",
    )(o_gated, u_act, proj, proj, x2d, mod6, g2, w_dn_o, w_cf_o, w_out)


def _ffn_kernel(hn_ref, x1_ref, mod_ref, gf_ref, wg_ref, wu_ref, cw_ref, wd_ref, out_ref,
                gbuf, carry, *, tiles_per_seq):
    i = pl.program_id(0)
    j = pl.program_id(1)
    nf = pl.num_programs(1)
    TM = FFN_TM

    @pl.when(i == 0)
    def _():
        carry[j] = jnp.zeros((SUBLANES, FFN_TF), F32)

    keep = (i % tiles_per_seq != 0).astype(F32)
    gbuf[0:SUBLANES, :] = carry[j] * keep

    def step(first, last):
        hs = []
        for r in range(0, TM, ROW_SUB):
            hn = hn_ref[r:r + ROW_SUB, :]
            gbuf[SUBLANES + r:SUBLANES + r + ROW_SUB, :] = _dot(hn, wg_ref[...])
            up = _dot(hn, wu_ref[...])
            acc = None
            for tap in range(FFN_CONV):
                o = SUBLANES + r - (FFN_CONV - 1) + tap
                term = gbuf[o:o + ROW_SUB, :] * cw_ref[tap:tap + 1, :]
                acc = term if acc is None else acc + term
            hs.append((_silu(acc) * up).astype(BF16))
        for idx, r in enumerate(range(0, TM, ROW_SUB)):
            rs = slice(r, r + ROW_SUB)
            part = _dot(hs[idx], wd_ref[...])
            total = part if first else out_ref[rs, :] + part
            if last:
                x2 = x1_ref[rs, :] + mod_ref[5:6, :] * total
                ms = jnp.mean(x2 * x2, axis=-1, keepdims=True)
                total = x2 * lax.rsqrt(ms + EPS) * gf_ref[...]
            out_ref[rs, :] = total
        carry[j] = gbuf[TM:TM + SUBLANES, :]

    pl.when(j == 0)(functools.partial(step, True, False))
    pl.when(jnp.logical_and(j > 0, j < nf - 1))(functools.partial(step, False, False))
    pl.when(j == nf - 1)(functools.partial(step, False, True))


def _ffn(hn2, x1, mod6, gf, w_up, conv_w, w_down, seq):
    m, d = x1.shape
    nf = FFN_DIM // FFN_TF
    tiles_per_seq = seq // FFN_TM
    return pl.pallas_call(
        functools.partial(_ffn_kernel, tiles_per_seq=tiles_per_seq),
        grid=(m // FFN_TM, nf),
        in_specs=[pl.BlockSpec((FFN_TM, d), lambda i, j: (i, 0)),
                  pl.BlockSpec((FFN_TM, d), lambda i, j: (i, 0)),
                  pl.BlockSpec((None, 6, d), lambda i, j: (i // tiles_per_seq, 0, 0)),
                  pl.BlockSpec((1, d), lambda i, j: (0, 0)),
                  pl.BlockSpec((d, FFN_TF), lambda i, j: (0, j)),
                  pl.BlockSpec((d, FFN_TF), lambda i, j: (0, nf + j)),
                  pl.BlockSpec((FFN_CONV, FFN_TF), lambda i, j: (0, j)),
                  pl.BlockSpec((FFN_TF, d), lambda i, j: (j, 0))],
        out_specs=pl.BlockSpec((FFN_TM, d), lambda i, j: (i, 0)),
        out_shape=jax.ShapeDtypeStruct((m, d), F32),
        scratch_shapes=[pltpu.VMEM((FFN_TM + SUBLANES, FFN_TF), F32),
                        pltpu.VMEM((nf, SUBLANES, FFN_TF), F32)],
        compiler_params=_params(("arbitrary", "arbitrary")),
        name="ffn",
    )(hn2, x1, mod6, gf, w_up, w_up, conv_w, w_down)


def kernel(x, c, w_ada, b_ada, norm1_g, w_in, dn_conv_w, dn_a_log, dn_dt_bias, dn_norm_g, dn_w_o,
           cf_conv_w, cf_ln_g, cf_ln_b, cf_w_o, w_out, norm2_g, ffn_w_up, ffn_conv_w, ffn_w_down,
           final_norm_g):
    bsz, seq, d = x.shape
    assert w_ada.shape[0] == 1, "single-layer block"
    x2d = x.reshape(bsz * seq, d)
    for l in range(1):
        mod6 = _ada(c, w_ada[l], b_ada[l]).reshape(bsz, 6, d)
        w_in_t = jnp.transpose(w_in[l])
        w_main, wba = _repack_w_in(w_in_t)
        proj, ba = _inproj(x2d, mod6, norm1_g[l].reshape(1, d), w_main, wba, dn_conv_w[l], seq)
        o_gated = _deltanet(proj, ba, dn_a_log[l], dn_dt_bias[l], dn_norm_g[l], bsz, seq)
        u_act, w_dn_o, w_cf_o, w_out_b, w_up, w_down = _conformer(
            proj, cf_conv_w[l], cf_ln_g[l], cf_ln_b[l], bsz, seq, l, [dn_w_o, cf_w_o, w_out, ffn_w_up, ffn_w_down])
        x1, hn2 = _merge(o_gated, u_act, proj, x2d, mod6, norm2_g[l].reshape(1, d), w_dn_o, w_cf_o, w_out_b, seq)
        x2d = _ffn(hn2, x1, mod6, final_norm_g.reshape(1, d), w_up, ffn_conv_w[l], w_down, seq)
    return x2d.reshape(bsz, seq, d)
```

```python
import functools

import jax
import jax.numpy as jnp
from jax import lax
from jax.experimental import pallas as pl
from jax.experimental.pallas import tpu as pltpu

F32 = jnp.float32
BF16 = jnp.bfloat16

D_MODEL = 2048
DN_HEADS = 8
DN_HEAD_DIM = 128
DN_WIDTH = DN_HEADS * DN_HEAD_DIM
DN_CONV = 4
DN_CHUNK = 64
CF_WIDTH = 1024
CF_KERNEL = 31
FFN_DIM = 5632
FFN_CONV = 3
EPS = 1e-6

LANES = 128
SUBLANES = 8
VMEM_LIMIT = 60000 * 1024

COL_Z = 3 * DN_WIDTH
COL_GLU = COL_Z + DN_WIDTH
COL_GATE_A = COL_GLU + 2 * CF_WIDTH
COL_GATE_B = COL_GATE_A + D_MODEL
N_MAIN = COL_GATE_B + D_MODEL

ROW_SUB = 256
REPACK_COLS = 512
ADA_TN = 1024
INPROJ_TM = 1024
INPROJ_TN = 1024
DN_T = 256
DN_ROWS = 2
DN_SOLVE_BLOCK = 16
CF_T = 256
CF_HALO = 32
MERGE_TM = 512
FFN_TM = 512
FFN_TF = 512


def _dot(a, b):
    return jnp.dot(a, b, preferred_element_type=F32)


def _sigmoid(x):
    return 0.5 * jnp.tanh(0.5 * x) + 0.5


def _silu(x):
    h = 0.5 * x
    return h * jnp.tanh(h) + h


def _params(sem):
    return pltpu.CompilerParams(dimension_semantics=sem, vmem_limit_bytes=VMEM_LIMIT)


def _ada_kernel(c_ref, w_ref, b_ref, o_ref):
    c = c_ref[...]
    cact = _silu(c).astype(BF16)
    o_ref[...] = _dot(cact, w_ref[...].astype(BF16)) + b_ref[...]


def _ada(c, w, b):
    bsz, d = c.shape
    n = w.shape[1]
    return pl.pallas_call(
        _ada_kernel,
        grid=(n // ADA_TN,),
        in_specs=[pl.BlockSpec((bsz, d), lambda j: (0, 0)),
                  pl.BlockSpec((d, ADA_TN), lambda j: (0, j)),
                  pl.BlockSpec((1, ADA_TN), lambda j: (0, j))],
        out_specs=pl.BlockSpec((bsz, ADA_TN), lambda j: (0, j)),
        out_shape=jax.ShapeDtypeStruct((bsz, n), F32),
        compiler_params=_params(("arbitrary",)),
        name="ada",
    )(c, w, b.reshape(1, n))


def _repack_kernel(wt_ref, wt_ba_ref, main_ref, ba_ref):
    main_ref[...] = wt_ref[...].T.astype(BF16)

    @pl.when(pl.program_id(0) == 0)
    def _():
        lane = lax.broadcasted_iota(jnp.int32, ba_ref.shape, 1)
        ba_ref[...] = jnp.where(lane < 2 * DN_HEADS, wt_ba_ref[...].T, 0.0).astype(BF16)


def _repack_w_in(w_t):
    n, d = w_t.shape
    lo, skip = COL_GLU, 2 * DN_HEADS
    assert lo % REPACK_COLS == 0 and skip % SUBLANES == 0
    return pl.pallas_call(
        _repack_kernel,
        grid=(N_MAIN // REPACK_COLS,),
        in_specs=[pl.BlockSpec((pl.Element(REPACK_COLS), pl.Element(d)),
                               lambda c: ((c * (REPACK_COLS // SUBLANES)
                                           + jnp.where(c >= lo // REPACK_COLS, skip // SUBLANES, 0)) * SUBLANES, 0)),
                  pl.BlockSpec((LANES, d), lambda c: (lo // LANES, 0))],
        out_specs=[pl.BlockSpec((d, REPACK_COLS), lambda c: (0, c)),
                   pl.BlockSpec((d, LANES), lambda c: (0, 0))],
        out_shape=[jax.ShapeDtypeStruct((d, N_MAIN), BF16),
                   jax.ShapeDtypeStruct((d, LANES), BF16)],
        compiler_params=_params(("arbitrary",)),
        name="repack",
    )(w_t, w_t)


def _modulated_rmsnorm(x, g, shift, scale):
    ms = jnp.mean(x * x, axis=-1, keepdims=True)
    y = x * lax.rsqrt(ms + EPS) * g
    return y * (1.0 + scale) + shift


def _inproj_kernel(x_ref, mod_ref, g_ref, w_ref, wba_ref, cw_ref, out_ref, ba_ref, hn_ref, cbuf, obuf, carry,
                   *, tiles_per_seq):
    i = pl.program_id(0)
    j = pl.program_id(1)
    TM = INPROJ_TM

    @pl.when(jnp.logical_and(i == 0, j == 0))
    def _():
        carry[...] = jnp.zeros_like(carry)

    def conv_tile(slot, l2_scale, first=False):
        keep = (i % tiles_per_seq != 0).astype(F32)
        cbuf[:, 0:SUBLANES, :] = carry[slot] * keep
        n = ROW_SUB // SUBLANES
        for r in range(0, TM, ROW_SUB):
            if first:
                hn = _modulated_rmsnorm(x_ref[r:r + ROW_SUB, :], g_ref[...], mod_ref[0:1, :], mod_ref[1:2, :])
                hn_ref[r:r + ROW_SUB, :] = hn.astype(BF16)
            res = _dot(hn_ref[r:r + ROW_SUB, :], w_ref[...])
            for cb in range(INPROJ_TN // LANES):
                cs = slice(cb * LANES, (cb + 1) * LANES)
                col = cbuf.at[cb]
                col[SUBLANES + r:SUBLANES + r + ROW_SUB, :] = res[:, cs]
                base = SUBLANES + r - (DN_CONV - 1)
                acc = [None] * SUBLANES
                for s in range(SUBLANES - 1 + DN_CONV):
                    xs = col[pl.ds(base + s, n, stride=SUBLANES), :]
                    for p in range(SUBLANES):
                        tap = s - p
                        if 0 <= tap < DN_CONV:
                            term = xs * cw_ref[tap:tap + 1, cs]
                            acc[p] = term if acc[p] is None else acc[p] + term
                for p in range(SUBLANES):
                    y = _silu(acc[p])
                    if l2_scale is not None:
                        y = y * (lax.rsqrt(jnp.sum(y * y, axis=-1, keepdims=True) + EPS) * l2_scale)
                    obuf.at[cb][pl.ds(r + p, n, stride=SUBLANES), :] = y
                out_ref[r:r + ROW_SUB, cs] = obuf[cb, r:r + ROW_SUB, :].astype(BF16)
        carry[slot] = cbuf[:, TM:TM + SUBLANES, :]
        if first:
            ba_ref[...] = _dot(hn_ref[...], wba_ref[...])

    pl.when(j == 0)(functools.partial(conv_tile, 0, DN_HEAD_DIM ** -0.5, first=True))
    pl.when(j == 1)(functools.partial(conv_tile, 1, 1.0))
    pl.when(j == 2)(functools.partial(conv_tile, 2, None))

    @pl.when(j >= 3)
    def _():
        for r in range(0, TM, ROW_SUB):
            out_ref[r:r + ROW_SUB, :] = _dot(hn_ref[r:r + ROW_SUB, :], w_ref[...]).astype(BF16)


def _inproj(x2d, mod6, g1, w_main, wba, conv_w, seq):
    m, d = x2d.shape
    n = w_main.shape[1]
    assert INPROJ_TN == DN_WIDTH and DN_HEAD_DIM == LANES, "q / k / v: one column tile each, one head per lane tile"
    tiles_per_seq = seq // INPROJ_TM
    n_conv = 3
    nb = INPROJ_TN // LANES
    return pl.pallas_call(
        functools.partial(_inproj_kernel, tiles_per_seq=tiles_per_seq),
        grid=(m // INPROJ_TM, n // INPROJ_TN),
        in_specs=[pl.BlockSpec((INPROJ_TM, d), lambda i, j: (i, 0)),
                  pl.BlockSpec((None, 6, d), lambda i, j: (i // tiles_per_seq, 0, 0)),
                  pl.BlockSpec((1, d), lambda i, j: (0, 0)),
                  pl.BlockSpec((d, INPROJ_TN), lambda i, j: (0, j)),
                  pl.BlockSpec((d, LANES), lambda i, j: (0, 0)),
                  pl.BlockSpec((DN_CONV, INPROJ_TN), lambda i, j: (0, jnp.minimum(j, n_conv - 1)))],
        out_specs=[pl.BlockSpec((INPROJ_TM, INPROJ_TN), lambda i, j: (i, j)),
                   pl.BlockSpec((INPROJ_TM, LANES), lambda i, j: (i, 0))],
        out_shape=[jax.ShapeDtypeStruct((m, n), BF16),
                   jax.ShapeDtypeStruct((m, LANES), F32)],
        scratch_shapes=[pltpu.VMEM((INPROJ_TM, d), BF16),
                        pltpu.VMEM((nb, INPROJ_TM + SUBLANES, LANES), F32),
                        pltpu.VMEM((nb, INPROJ_TM, LANES), F32),
                        pltpu.VMEM((n_conv, nb, SUBLANES, LANES), F32)],
        compiler_params=_params(("arbitrary", "arbitrary")),
        name="inproj",
    )(x2d, mod6, g1, w_main, wba, conv_w)


def _split3(x):
    h1 = x.astype(BF16)
    r1 = x - h1.astype(F32)
    h2 = r1.astype(BF16)
    h3 = (r1 - h2.astype(F32)).astype(BF16)
    return h1, h2, h3


def _deltanet_kernel(q_ref, k_ref, v_ref, z_ref, ba_ref, alog_ref, dtb_ref, ng_ref, out_ref, state):
    t = pl.program_id(1)
    T, C, H, DH = DN_T, DN_CHUNK, DN_HEADS, DN_HEAD_DIM
    NC = T // C

    @pl.when(t == 0)
    def _():
        state[...] = jnp.zeros_like(state)

    row = lax.broadcasted_iota(jnp.int32, (T, T), 0)
    col = lax.broadcasted_iota(jnp.int32, (T, T), 1)
    same = (row // C) == (col // C)
    lower = jnp.where(jnp.logical_and(same, col <= row), 1.0, 0.0).astype(BF16)
    upper = jnp.where(jnp.logical_and(same, row <= col), 1.0, 0.0).astype(BF16)

    def gates(rb):
        ba = ba_ref[rb]
        beta_all = _sigmoid(ba)
        sp_in = ba + dtb_ref[...]
        softplus = jnp.maximum(sp_in, 0.0) + jnp.log(1.0 + jnp.exp(-jnp.abs(sp_in)))
        g_all = -jnp.exp(alog_ref[...]) * softplus
        g1, g2, g3 = _split3(g_all)
        gc_col_all = _dot(lower, g1) + _dot(lower, g2) + _dot(lower, g3)
        g_rows = g_all.T[0:2 * H, :]
        r1, r2, r3 = _split3(g_rows)
        gc_rows = _dot(r1, upper) + _dot(r2, upper) + _dot(r3, upper)
        return beta_all, gc_col_all, gc_rows, jnp.exp(gc_rows)

    wrow = lax.broadcasted_iota(jnp.int32, (C, T), 0)
    wcol = lax.broadcasted_iota(jnp.int32, (C, T), 1) % C
    causal_w = wcol <= wrow
    strict_w = wcol < wrow
    assert C == 4 * DN_SOLVE_BLOCK, "the blocked inverse below expands (I + m)^-1 to its 4 non-zero terms"
    diag_w = (wrow // DN_SOLVE_BLOCK) == (wcol // DN_SOLVE_BLOCK)
    first_half = lax.broadcasted_iota(jnp.int32, (C, LANES), 1) < C

    def widen(x):
        cols = []
        for p in range(NC // 2):
            ls = slice(p * LANES, (p + 1) * LANES)
            cols.append(jnp.where(first_half, x[2 * p * C:(2 * p + 1) * C, ls],
                                  x[(2 * p + 1) * C:(2 * p + 2) * C, ls]))
        return jnp.concatenate(cols, axis=1)

    same_b = jnp.where(same, 1.0, 0.0).astype(BF16)

    def blockdiag(xw):
        return jnp.concatenate([xw.astype(BF16)] * NC, axis=0) * same_b

    def head_prep(rb, h, gate_vals, done):
        beta_all, gc_col_all, gc_rows, eg_rows = gate_vals
        hs = slice(h * DH, (h + 1) * DH)
        qb = q_ref[rb, :, hs]
        q = qb.astype(F32)
        k = k_ref[rb, :, hs].astype(F32)
        v = v_ref[rb, :, hs].astype(F32)
        beta = jnp.broadcast_to(beta_all[:, h:h + 1], (T, DH))
        gcol = jnp.broadcast_to(gc_col_all[:, H + h:H + h + 1], (T, LANES))
        egcol = jnp.exp(gcol)
        grow = gc_rows[H + h:H + h + 1, :]
        gcol_w = widen(jnp.concatenate([gcol] * (NC // 2), axis=1))
        dec_w = jnp.exp(jnp.where(causal_w, gcol_w - grow, -1e30))

        kb = k * beta
        kT = k.T
        yield
        mm = _dot(jnp.concatenate([kb.astype(BF16), qb], axis=0), kT.astype(BF16))
        yield
        a_w = jnp.where(strict_w, widen(mm[:T]) * dec_w, 0.0)
        attn_w = widen(mm[T:]) * dec_w

        d_w = jnp.where(diag_w, a_w, 0.0)
        n_w = a_w - d_w
        rd = -d_w
        p_w = _dot(d_w.astype(BF16), blockdiag(d_w))
        yield
        n_sq = 3
        for s in range(n_sq):
            pbd = blockdiag(p_w)
            if s < n_sq - 1:
                rp = _dot(jnp.concatenate([rd, p_w], axis=0).astype(BF16), pbd)
                rd = rd + p_w + rp[:C]
                p_w = rp[C:]
            else:
                rd = rd + p_w + _dot(rd.astype(BF16), pbd)
            yield
        m = n_w + _dot(rd.astype(BF16), blockdiag(n_w))
        yield
        mb = m.astype(BF16)
        m2 = _dot(mb, blockdiag(m))
        yield
        rm = m2 - m - _dot(mb, blockdiag(m2))
        yield
        r_w = rm + rd + _dot(rm.astype(BF16), blockdiag(rd))
        yield

        rhs = jnp.concatenate([v * beta, kb * egcol], axis=1)
        sol = rhs + _dot(blockdiag(r_w), rhs.astype(BF16))
        yield
        u0 = sol[:, :DH]
        qd = q * egcol
        wq = [jnp.concatenate([sol[c * C:(c + 1) * C, DH:], qd[c * C:(c + 1) * C]], axis=0).astype(BF16)
              for c in range(NC)]
        kd_t = kT * dec_w[C - 1:C, :]
        ak = [jnp.concatenate([attn_w[:, p * LANES:(p + 1) * LANES], kd_t[:, p * LANES:(p + 1) * LANES]],
                              axis=0).astype(BF16) for p in range(NC // 2)]
        eg = [eg_rows[H + h:H + h + 1, c * C + C - 1:c * C + C] for c in range(NC)]
        done[rb * H + h] = (u0, wq, ak, eg)

    def pair_diag(x1, x2):
        z = jnp.zeros_like(x1)
        return jnp.concatenate([jnp.concatenate([x1, z], axis=1), jnp.concatenate([z, x2], axis=1)], axis=0)

    def pair_recurrence(rb, hp, prep):
        hs = (2 * hp, 2 * hp + 1)
        pr = [prep[rb * H + h] for h in hs]
        st = [state[rb * H + h] for h in hs]
        outs = [[], []]
        for c in range(NC):
            res = _dot(jnp.concatenate([pr[0][1][c], pr[1][1][c]], axis=1),
                       pair_diag(st[0].astype(BF16), st[1].astype(BF16)))
            yield
            u_pairs = []
            for i in range(2):
                u = pr[i][0][c * C:(c + 1) * C] - res[:C, i * DH:(i + 1) * DH]
                zu = jnp.zeros_like(u)
                u_pairs.append(jnp.concatenate([u, zu] if c % 2 == 0 else [zu, u], axis=0).astype(BF16))
            res2 = _dot(jnp.concatenate([pr[0][2][c // 2], pr[1][2][c // 2]], axis=1),
                        pair_diag(u_pairs[0], u_pairs[1]))
            yield
            for i in range(2):
                ls = slice(i * DH, (i + 1) * DH)
                outs[i].append(res[C:, ls] + res2[:C, ls])
                st[i] = st[i] * pr[i][3][c] + res2[C:, ls]
        for i, h in enumerate(hs):
            state[rb * H + h] = st[i]
            o = jnp.concatenate(outs[i], axis=0)
            o = o * lax.rsqrt(jnp.mean(o * o, axis=-1, keepdims=True) + EPS) * ng_ref[...]
            zz = z_ref[rb, :, h * DH:(h + 1) * DH].astype(F32)
            out_ref[rb, :, h * DH:(h + 1) * DH] = (o * _silu(zz)).astype(BF16)

    def lockstep(gens):
        gens = list(gens)
        while gens:
            gens = [g for g in gens if next(g, True) is None]

    prep = [None] * (DN_ROWS * H)
    gate_vals = [gates(rb) for rb in range(DN_ROWS)]
    lockstep(head_prep(rb, h, gate_vals[rb], prep) for rb in range(DN_ROWS) for h in range(H))
    lockstep(pair_recurrence(rb, hp, prep) for rb in range(DN_ROWS) for hp in range(H // 2))


def _deltanet(proj, ba, a_log, dt_bias, norm_g, bsz, seq):
    nt = seq // DN_T
    w = DN_WIDTH
    pad = lambda p: jnp.zeros((1, LANES), F32).at[0, DN_HEADS:2 * DN_HEADS].set(p)
    proj3 = proj.reshape(bsz, seq, proj.shape[1])
    col = lambda c: pl.BlockSpec((DN_ROWS, DN_T, w), lambda b, t: (b, t, c))
    return pl.pallas_call(
        _deltanet_kernel,
        grid=(bsz // DN_ROWS, nt),
        in_specs=[col(0), col(1), col(2), col(COL_Z // w),
                  pl.BlockSpec((DN_ROWS, DN_T, LANES), lambda b, t: (b, t, 0)),
                  pl.BlockSpec((1, LANES), lambda b, t: (0, 0)),
                  pl.BlockSpec((1, LANES), lambda b, t: (0, 0)),
                  pl.BlockSpec((1, DN_HEAD_DIM), lambda b, t: (0, 0))],
        out_specs=pl.BlockSpec((DN_ROWS, DN_T, w), lambda b, t: (b, t, 0)),
        out_shape=jax.ShapeDtypeStruct((bsz, seq, w), BF16),
        scratch_shapes=[pltpu.VMEM((DN_ROWS * DN_HEADS, DN_HEAD_DIM, DN_HEAD_DIM), F32)],
        compiler_params=_params(("arbitrary", "arbitrary")),
        name="deltanet",
    )(proj3, proj3, proj3, proj3, ba.reshape(bsz, seq, LANES), pad(a_log), pad(dt_bias),
      norm_g.reshape(1, DN_HEAD_DIM)).reshape(bsz * seq, w)


def _conformer_kernel(val_ref, gl_ref, cw_ref, lg_ref, lb_ref, *rest, n_cast):
    out_ref = rest[n_cast]
    ubuf, cbuf = rest[-2:]
    for src, dst in zip(rest[:n_cast], rest[n_cast + 1:2 * n_cast + 1]):
        dst[...] = src[...].astype(BF16)
    t = pl.program_id(1)
    T = CF_T
    NB = CF_WIDTH // LANES

    @pl.when(t == 0)
    def _():
        ubuf[:, 0:CF_HALO, :] = jnp.zeros((NB, CF_HALO, LANES), F32)

    @pl.when(t > 0)
    def _():
        ubuf[:, 0:CF_HALO, :] = ubuf[:, T:T + CF_HALO, :]

    off = CF_HALO - (CF_KERNEL - 1)
    n = T // SUBLANES
    for cb in range(NB):
        cs = slice(cb * LANES, (cb + 1) * LANES)
        ucol = ubuf.at[cb]
        ucol[CF_HALO:CF_HALO + T, :] = val_ref[:, cs].astype(F32) * _sigmoid(gl_ref[:, cs].astype(F32))
        acc = [None] * SUBLANES
        for s in range(off, off + SUBLANES - 1 + CF_KERNEL):
            xs = ucol[pl.ds(s, n, stride=SUBLANES), :]
            for p in range(SUBLANES):
                j = s - off - p
                if 0 <= j < CF_KERNEL:
                    term = xs * cw_ref[j:j + 1, cs]
                    acc[p] = term if acc[p] is None else acc[p] + term
        for p in range(SUBLANES):
            cbuf.at[cb][pl.ds(p, n, stride=SUBLANES), :] = acc[p]

    y = jnp.concatenate([cbuf[cb] for cb in range(NB)], axis=1)
    mu = jnp.mean(y, axis=-1, keepdims=True)
    yc = y - mu
    var = jnp.mean(yc * yc, axis=-1, keepdims=True)
    yn = yc * lax.rsqrt(var + EPS) * lg_ref[...] + lb_ref[...]
    out_ref[...] = _silu(yn).astype(BF16)


def _conformer(proj, conv_w, ln_g, ln_b, bsz, seq, layer, weights):
    nt = seq // CF_T
    n_steps = bsz * nt
    tok = lambda b, t: b * nt + t
    bf16_sublanes = 2 * SUBLANES
    w_in_specs, w_out_specs, w_out_shapes = [], [], []
    for w in weights:
        _, rows, cols = w.shape
        share = 1
        while rows * share % n_steps or (rows * share // n_steps) % bf16_sublanes:
            share *= 2
        slab = rows * share // n_steps
        w_in_specs.append(pl.BlockSpec((None, slab, cols), lambda b, t, share=share: (layer, tok(b, t) // share, 0)))
        w_out_specs.append(pl.BlockSpec((slab, cols), lambda b, t, share=share: (tok(b, t) // share, 0)))
        w_out_shapes.append(jax.ShapeDtypeStruct((rows, cols), BF16))
    return pl.pallas_call(
        functools.partial(_conformer_kernel, n_cast=len(weights)),
        grid=(bsz, nt),
        in_specs=[pl.BlockSpec((CF_T, CF_WIDTH), lambda b, t: (tok(b, t), COL_GLU // CF_WIDTH)),
                  pl.BlockSpec((CF_T, CF_WIDTH), lambda b, t: (tok(b, t), COL_GLU // CF_WIDTH + 1)),
                  pl.BlockSpec((CF_KERNEL, CF_WIDTH), lambda b, t: (0, 0)),
                  pl.BlockSpec((1, CF_WIDTH), lambda b, t: (0, 0)),
                  pl.BlockSpec((1, CF_WIDTH), lambda b, t: (0, 0))] + w_in_specs,
        out_specs=[pl.BlockSpec((CF_T, CF_WIDTH), lambda b, t: (tok(b, t), 0))] + w_out_specs,
        out_shape=[jax.ShapeDtypeStruct((bsz * seq, CF_WIDTH), BF16)] + w_out_shapes,
        scratch_shapes=[pltpu.VMEM((CF_WIDTH // LANES, CF_T + CF_HALO, LANES), F32),
                        pltpu.VMEM((CF_WIDTH // LANES, CF_T, LANES), F32)],
        compiler_params=_params(("arbitrary", "arbitrary")),
        name="conformer",
    )(proj, proj, conv_w, ln_g.reshape(1, CF_WIDTH), ln_b.reshape(1, CF_WIDTH), *weights)


def _merge_kernel(o_ref, u_ref, ga_ref, gb_ref, x_ref, mod_ref, g2_ref, wdo_ref, wco_ref, wout_ref,
                  x1_ref, hn2_ref):
    merged = []
    for r in range(0, MERGE_TM, ROW_SUB):
        rs = slice(r, r + ROW_SUB)
        branch_a = _dot(o_ref[rs, :], wdo_ref[...])
        branch_b = _dot(u_ref[rs, :], wco_ref[...])
        merged.append((_sigmoid(ga_ref[rs, :].astype(F32)) * branch_a
                       + _sigmoid(gb_ref[rs, :].astype(F32)) * branch_b).astype(BF16))
    for idx, r in enumerate(range(0, MERGE_TM, ROW_SUB)):
        rs = slice(r, r + ROW_SUB)
        mix = _dot(merged[idx], wout_ref[...])
        x1 = x_ref[rs, :] + mod_ref[2:3, :] * mix
        x1_ref[rs, :] = x1
        hn2 = _modulated_rmsnorm(x1, g2_ref[...], mod_ref[3:4, :], mod_ref[4:5, :])
        hn2_ref[rs, :] = hn2.astype(BF16)


def _merge(o_gated, u_act, proj, x2d, mod6, g2, w_dn_o, w_cf_o, w_out, seq):
    m, d = x2d.shape
    tiles_per_seq = seq // MERGE_TM
    const = lambda shape: pl.BlockSpec(shape, lambda i: (0, 0), pipeline_mode=pl.Buffered(1))
    return pl.pallas_call(
        _merge_kernel,
        grid=(m // MERGE_TM,),
        in_specs=[pl.BlockSpec((MERGE_TM, DN_WIDTH), lambda i: (i, 0)),
                  pl.BlockSpec((MERGE_TM, CF_WIDTH), lambda i: (i, 0)),
                  pl.BlockSpec((MERGE_TM, d), lambda i: (i, COL_GATE_A // d)),
                  pl.BlockSpec((MERGE_TM, d), lambda i: (i, COL_GATE_B // d)),
                  pl.BlockSpec((MERGE_TM, d), lambda i: (i, 0)),
                  pl.BlockSpec((None, 6, d), lambda i: (i // tiles_per_seq, 0, 0)),
                  pl.BlockSpec((1, d), lambda i: (0, 0)),
                  const((DN_WIDTH, d)),
                  const((CF_WIDTH, d)),
                  const((d, d))],
        out_specs=[pl.BlockSpec((MERGE_TM, d), lambda i: (i, 0)),
                   pl.BlockSpec((MERGE_TM, d), lambda i: (i, 0))],
        out_shape=[jax.ShapeDtypeStruct((m, d), F32),
                   jax.ShapeDtypeStruct((m, d), BF16)],
        compiler_params=_params(("arbitrary",)),
        name="merge",
    )(o_gated, u_act, proj, proj, x2d, mod6, g2, w_dn_o, w_cf_o, w_out)


def _ffn_kernel(hn_ref, x1_ref, mod_ref, gf_ref, wg_ref, wu_ref, cw_ref, wd_ref, out_ref,
                gbuf, carry, *, tiles_per_seq):
    i = pl.program_id(0)
    j = pl.program_id(1)
    nf = pl.num_programs(1)
    TM = FFN_TM

    @pl.when(j == 0)
    def _():
        out_ref[...] = jnp.zeros_like(out_ref)

    @pl.when(i == 0)
    def _():
        carry[j] = jnp.zeros((SUBLANES, FFN_TF), F32)

    keep = (i % tiles_per_seq != 0).astype(F32)
    gbuf[0:SUBLANES, :] = carry[j] * keep

    hs = []
    for r in range(0, TM, ROW_SUB):
        hn = hn_ref[r:r + ROW_SUB, :]
        gbuf[SUBLANES + r:SUBLANES + r + ROW_SUB, :] = _dot(hn, wg_ref[...])
        up = _dot(hn, wu_ref[...])
        acc = None
        for tap in range(FFN_CONV):
            o = SUBLANES + r - (FFN_CONV - 1) + tap
            term = gbuf[o:o + ROW_SUB, :] * cw_ref[tap:tap + 1, :]
            acc = term if acc is None else acc + term
        hs.append((_silu(acc) * up).astype(BF16))
    for idx, r in enumerate(range(0, TM, ROW_SUB)):
        out_ref[r:r + ROW_SUB, :] += _dot(hs[idx], wd_ref[...])

    carry[j] = gbuf[TM:TM + SUBLANES, :]

    @pl.when(j == nf - 1)
    def _():
        x2 = x1_ref[...] + mod_ref[5:6, :] * out_ref[...]
        ms = jnp.mean(x2 * x2, axis=-1, keepdims=True)
        out_ref[...] = x2 * lax.rsqrt(ms + EPS) * gf_ref[...]


def _ffn(hn2, x1, mod6, gf, w_up, conv_w, w_down, seq):
    m, d = x1.shape
    nf = FFN_DIM // FFN_TF
    tiles_per_seq = seq // FFN_TM
    return pl.pallas_call(
        functools.partial(_ffn_kernel, tiles_per_seq=tiles_per_seq),
        grid=(m // FFN_TM, nf),
        in_specs=[pl.BlockSpec((FFN_TM, d), lambda i, j: (i, 0)),
                  pl.BlockSpec((FFN_TM, d), lambda i, j: (i, 0)),
                  pl.BlockSpec((None, 6, d), lambda i, j: (i // tiles_per_seq, 0, 0)),
                  pl.BlockSpec((1, d), lambda i, j: (0, 0)),
                  pl.BlockSpec((d, FFN_TF), lambda i, j: (0, j)),
                  pl.BlockSpec((d, FFN_TF), lambda i, j: (0, nf + j)),
                  pl.BlockSpec((FFN_CONV, FFN_TF), lambda i, j: (0, j)),
                  pl.BlockSpec((FFN_TF, d), lambda i, j: (j, 0))],
        out_specs=pl.BlockSpec((FFN_TM, d), lambda i, j: (i, 0)),
        out_shape=jax.ShapeDtypeStruct((m, d), F32),
        scratch_shapes=[pltpu.VMEM((FFN_TM + SUBLANES, FFN_TF), F32),
                        pltpu.VMEM((nf, SUBLANES, FFN_TF), F32)],
        compiler_params=_params(("arbitrary", "arbitrary")),
        name="ffn",
    )(hn2, x1, mod6, gf, w_up, w_up, conv_w, w_down)


def kernel(x, c, w_ada, b_ada, norm1_g, w_in, dn_conv_w, dn_a_log, dn_dt_bias, dn_norm_g, dn_w_o,
           cf_conv_w, cf_ln_g, cf_ln_b, cf_w_o, w_out, norm2_g, ffn_w_up, ffn_conv_w, ffn_w_down,
           final_norm_g):
    bsz, seq, d = x.shape
    assert w_ada.shape[0] == 1, "single-layer block"
    x2d = x.reshape(bsz * seq, d)
    for l in range(1):
        mod6 = _ada(c, w_ada[l], b_ada[l]).reshape(bsz, 6, d)
        w_in_t = jnp.transpose(w_in[l])
        w_main, wba = _repack_w_in(w_in_t)
        proj, ba = _inproj(x2d, mod6, norm1_g[l].reshape(1, d), w_main, wba, dn_conv_w[l], seq)
        o_gated = _deltanet(proj, ba, dn_a_log[l], dn_dt_bias[l], dn_norm_g[l], bsz, seq)
        u_act, w_dn_o, w_cf_o, w_out_b, w_up, w_down = _conformer(
            proj, cf_conv_w[l], cf_ln_g[l], cf_ln_b[l], bsz, seq, l, [dn_w_o, cf_w_o, w_out, ffn_w_up, ffn_w_down])
        x1, hn2 = _merge(o_gated, u_act, proj, x2d, mod6, norm2_g[l].reshape(1, d), w_dn_o, w_cf_o, w_out_b, seq)
        x2d = _ffn(hn2, x1, mod6, final_norm_g.reshape(1, d), w_up, ffn_conv_w[l], w_down, seq)
    return x2d.reshape(bsz, seq, d)
```
